```python
import jax, jax.numpy as jnp
from jax import lax
import numpy as np

D_MODEL = 2048
BATCH = 2
SEQ = 8192
DEPTH = 4

GRID_W = 64
EPS = 1e-6
D_SSM = D_MODEL
SSM_HEAD_DIM = 64
SSM_HEADS = D_SSM // SSM_HEAD_DIM
SSM_GROUPS = 8
SSM_STATE = 128
SSM_CONV = 5
SSM_CHUNK = 128
SSM_CONV_CH = D_SSM + 2 * SSM_GROUPS * SSM_STATE
D_NA = D_MODEL
NA_HEAD_DIM = 128
NA_HEADS = D_NA // NA_HEAD_DIM
NA_WIN_ROWS = 8
NA_WIN_COLS = 16
D_CONV = 2 * D_MODEL
CONV_WIDTH = 31
PLE_DIM = 256

N_EVEN = (DEPTH + 1) // 2
N_ODD = DEPTH // 2
EVEN_SPLITS = [D_SSM, SSM_CONV_CH, 2 * SSM_HEADS, D_NA, D_NA, D_NA, D_NA]
EVEN_IN = sum(EVEN_SPLITS)
ODD_IN = 3 * D_CONV

kernel_name = "hybrid_ssd_natten_conformer_encoder"


def rmsnorm(x, w, eps=EPS):
    xf = x.astype(jnp.float32)
    y = xf * lax.rsqrt(jnp.mean(xf * xf, axis=-1, keepdims=True) + eps)
    return (y * w.astype(jnp.float32)).astype(x.dtype)


def layernorm(x, w, b, eps=EPS):
    xf = x.astype(jnp.float32)
    mu = jnp.mean(xf, axis=-1, keepdims=True)
    xc = xf - mu
    y = xc * lax.rsqrt(jnp.mean(xc * xc, axis=-1, keepdims=True) + eps)
    return (y * w.astype(jnp.float32) + b.astype(jnp.float32)).astype(x.dtype)


def depthwise_conv(x, w, b):
    k = w.shape[0]
    y = lax.conv_general_dilated(
        x, w[:, None, :].astype(x.dtype), window_strides=(1,),
        padding=[(k // 2, k - 1 - k // 2)],
        dimension_numbers=("NWC", "WIO", "NWC"),
        feature_group_count=x.shape[-1])
    return y + b.astype(x.dtype)


def segsum(a):
    t = a.shape[-1]
    a_rep = jnp.broadcast_to(a[..., :, None], a.shape + (t,))
    strict = jnp.tril(jnp.ones((t, t), bool), -1)
    cs = jnp.cumsum(jnp.where(strict, a_rep, 0.0), axis=-2)
    return jnp.where(jnp.tril(jnp.ones((t, t), bool)), cs, -jnp.inf)


def ssd_chunked(x, dt, a, bm, cm):
    b, l, h, p = x.shape
    g, n = bm.shape[2], bm.shape[3]
    j = h // g
    q = SSM_CHUNK
    c = l // q
    xdt = (x * dt[..., None]).reshape(b, c, q, g, j, p)
    da = (dt * a).reshape(b, c, q, g, j)
    a_cs = jnp.cumsum(da, axis=2)
    bc = bm.reshape(b, c, q, g, n)
    cc = cm.reshape(b, c, q, g, n)
    decay_in = jnp.exp(segsum(da.transpose(0, 3, 4, 1, 2)))
    scores = jnp.einsum("bclgn,bcsgn->bgcls", cc, bc)[:, :, None] * decay_in
    y_diag = jnp.einsum("bgjcls,bcsgjp->bclgjp", scores, xdt)
    decay_states = jnp.exp(a_cs[:, :, -1:] - a_cs)
    states = jnp.einsum("bcsgn,bcsgjp->cbgjpn", bc, xdt * decay_states[..., None])
    chunk_decay = jnp.exp(jnp.moveaxis(a_cs[:, :, -1], 1, 0))

    def step(carry, inp):
        st, dec = inp
        return carry * dec[..., None, None] + st, carry

    _, prev = lax.scan(step, jnp.zeros(states.shape[1:], states.dtype), (states, chunk_decay))
    y_off = jnp.einsum("bclgn,cbgjpn->bclgjp", cc, prev) * jnp.exp(a_cs)[..., None]
    return (y_diag + y_off).reshape(b, l, h, p)


def ssd_branch(z, xbc, dt_raw, conv_w, conv_b, dt_bias_f, dt_bias_b, a_log_f, a_log_b, d_skip, gnorm_w):
    b, l, _ = z.shape
    xbc = jax.nn.silu(depthwise_conv(xbc, conv_w, conv_b)).astype(jnp.float32)
    xs, bm, cm = jnp.split(xbc, [D_SSM, D_SSM + SSM_GROUPS * SSM_STATE], axis=-1)
    xs = xs.reshape(b, l, SSM_HEADS, SSM_HEAD_DIM)
    bm = bm.reshape(b, l, SSM_GROUPS, SSM_STATE)
    cm = cm.reshape(b, l, SSM_GROUPS, SSM_STATE)
    dt_f, dt_b = jnp.split(dt_raw.astype(jnp.float32), 2, axis=-1)
    dt_f = jax.nn.softplus(dt_f + dt_bias_f.astype(jnp.float32))
    dt_b = jax.nn.softplus(dt_b + dt_bias_b.astype(jnp.float32))
    a_f = -jnp.exp(a_log_f.astype(jnp.float32))
    a_b = -jnp.exp(a_log_b.astype(jnp.float32))
    flip = lambda t: jnp.flip(t, axis=1)
    y_f = ssd_chunked(xs, dt_f, a_f, bm, cm)
    y_b = flip(ssd_chunked(flip(xs), flip(dt_b), a_b, flip(bm), flip(cm)))
    y = y_f + y_b + d_skip.astype(jnp.float32)[:, None] * xs
    y = y.reshape(b, l, D_SSM) * jax.nn.silu(z.astype(jnp.float32))
    yg = y.reshape(b, l, SSM_GROUPS, D_SSM // SSM_GROUPS)
    yg = yg * lax.rsqrt(jnp.mean(yg * yg, axis=-1, keepdims=True) + EPS)
    return (yg.reshape(b, l, D_SSM) * gnorm_w.astype(jnp.float32)).astype(z.dtype)


def neighbourhood_attention(q, k, v, rpb):
    b, s_len, _ = q.shape
    rows = s_len // GRID_W
    kr = min(NA_WIN_ROWS, rows)
    grid = (b, rows, GRID_W, NA_HEADS, NA_HEAD_DIM)
    qg, kg, vg = q.reshape(grid), k.reshape(grid), v.reshape(grid)
    cols = jnp.arange(GRID_W)
    col_start = jnp.clip(cols - NA_WIN_COLS // 2, 0, GRID_W - NA_WIN_COLS)
    col_valid = (cols[None, :] >= col_start[:, None]) & (cols[None, :] < col_start[:, None] + NA_WIN_COLS)
    col_off = jnp.clip(cols[None, :] - cols[:, None], -(NA_WIN_COLS - 1), NA_WIN_COLS - 1) + NA_WIN_COLS - 1
    rpb_cols = rpb.astype(jnp.float32)[:, :, col_off]
    scale = NA_HEAD_DIM ** -0.5

    def row_attend(args):
        q_row, r = args
        rs = jnp.clip(r - kr // 2, 0, rows - kr)
        k_rows = lax.dynamic_slice_in_dim(kg, rs, kr, axis=1).astype(jnp.float32)
        v_rows = lax.dynamic_slice_in_dim(vg, rs, kr, axis=1).astype(jnp.float32)
        s = jnp.einsum("bqhd,bikhd->bhqik", q_row.astype(jnp.float32), k_rows) * scale
        row_off = rs + jnp.arange(kr) - r + NA_WIN_ROWS - 1
        bias = jnp.take(rpb_cols, row_off, axis=1).transpose(0, 2, 1, 3)
        s = jnp.where(col_valid[None, None, :, None, :], s + bias[None], -jnp.inf)
        pr = jax.nn.softmax(s.reshape(b, NA_HEADS, GRID_W, kr * GRID_W), axis=-1).reshape(s.shape)
        return jnp.einsum("bhqik,bikhd->bqhd", pr, v_rows).astype(q_row.dtype)

    out = lax.map(row_attend, (jnp.moveaxis(qg, 1, 0), jnp.arange(rows)))
    return jnp.moveaxis(out, 0, 1).reshape(b, s_len, D_NA)


def even_mixer(hn, w_in, conv_w, conv_b, dt_bias_f, dt_bias_b, a_log_f, a_log_b, d_skip, gnorm_w, rpb, w_out):
    u = hn @ w_in
    z, xbc, dt_raw, q, k, v, g = jnp.split(u, np.cumsum(EVEN_SPLITS)[:-1].tolist(), axis=-1)
    y_ssd = ssd_branch(z, xbc, dt_raw, conv_w, conv_b, dt_bias_f, dt_bias_b, a_log_f, a_log_b, d_skip, gnorm_w)
    y_na = neighbourhood_attention(q, k, v, rpb) * jax.nn.silu(g)
    return jnp.concatenate([y_ssd, y_na], axis=-1) @ w_out


def conv_module(hn, w_in, dw_w, dw_b, ln_w, ln_b, w_out):
    a, a_gate, g = jnp.split(hn @ w_in, 3, axis=-1)
    v = a * jax.nn.sigmoid(a_gate)
    v = depthwise_conv(v, dw_w, dw_b)
    v = layernorm(v, ln_w, ln_b)
    v = jax.nn.silu(v) * jax.nn.silu(g)
    return v @ w_out


def per_layer_embedding(h, p_i, norm_w, w_gate, w_proj):
    gate = jax.nn.sigmoid(rmsnorm(h, norm_w) @ w_gate)
    return gate * (p_i @ w_proj)


def setup_inputs(seed: int = 0) -> dict:
    key = jax.random.key(seed)
    ks = iter(jax.random.split(key, 32))
    nrm = lambda shape, scale: jax.random.normal(next(ks), shape, jnp.float32) * scale
    gain = lambda shape: 1.0 + nrm(shape, 0.02)
    dt = jnp.exp(jax.random.uniform(next(ks), (2, N_EVEN, SSM_HEADS), jnp.float32,
                                    jnp.log(0.001), jnp.log(0.1)))
    dt_bias = dt + jnp.log(-jnp.expm1(-dt))
    a_log = jnp.log(jax.random.uniform(next(ks), (2, N_EVEN, SSM_HEADS), jnp.float32, 1.0, 16.0))
    return {
        "x": nrm((BATCH, SEQ, D_MODEL), 1.0),
        "p": nrm((DEPTH, BATCH, SEQ, PLE_DIM), 1.0),
        "ev_norm_w": gain((N_EVEN, D_MODEL)),
        "ev_w_in": nrm((N_EVEN, D_MODEL, EVEN_IN), D_MODEL ** -0.5),
        "ev_conv_w": nrm((N_EVEN, SSM_CONV, SSM_CONV_CH), SSM_CONV ** -0.5),
        "ev_conv_b": nrm((N_EVEN, SSM_CONV_CH), 0.02),
        "ev_dt_bias_f": dt_bias[0],
        "ev_dt_bias_b": dt_bias[1],
        "ev_a_log_f": a_log[0],
        "ev_a_log_b": a_log[1],
        "ev_d_skip": gain((N_EVEN, SSM_HEADS)),
        "ev_gnorm_w": gain((N_EVEN, D_SSM)),
        "ev_rpb": nrm((N_EVEN, NA_HEADS, 2 * NA_WIN_ROWS - 1, 2 * NA_WIN_COLS - 1), 0.1),
        "ev_w_out": nrm((N_EVEN, D_SSM + D_NA, D_MODEL), (D_SSM + D_NA) ** -0.5),
        "od_norm_w": gain((N_ODD, D_MODEL)),
        "od_w_in": nrm((N_ODD, D_MODEL, ODD_IN), D_MODEL ** -0.5),
        "od_dw_w": nrm((N_ODD, CONV_WIDTH, D_CONV), CONV_WIDTH ** -0.5),
        "od_dw_b": nrm((N_ODD, D_CONV), 0.02),
        "od_ln_w": gain((N_ODD, D_CONV)),
        "od_ln_b": nrm((N_ODD, D_CONV), 0.02),
        "od_w_out": nrm((N_ODD, D_CONV, D_MODEL), D_CONV ** -0.5),
        "ple_norm_w": gain((DEPTH, D_MODEL)),
        "ple_w_gate": nrm((DEPTH, D_MODEL, D_MODEL), D_MODEL ** -0.5),
        "ple_w_proj": nrm((DEPTH, PLE_DIM, D_MODEL), PLE_DIM ** -0.5),
        "final_norm_w": gain((D_MODEL,)),
    }


def reference(x, p, ev_norm_w, ev_w_in, ev_conv_w, ev_conv_b, ev_dt_bias_f, ev_dt_bias_b,
              ev_a_log_f, ev_a_log_b, ev_d_skip, ev_gnorm_w, ev_rpb, ev_w_out,
              od_norm_w, od_w_in, od_dw_w, od_dw_b, od_ln_w, od_ln_b, od_w_out,
              ple_norm_w, ple_w_gate, ple_w_proj, final_norm_w):
    h = x
    for i in range(DEPTH):
        e = i // 2
        if i % 2 == 0:
            h = h + even_mixer(rmsnorm(h, ev_norm_w[e]), ev_w_in[e], ev_conv_w[e], ev_conv_b[e],
                               ev_dt_bias_f[e], ev_dt_bias_b[e], ev_a_log_f[e], ev_a_log_b[e],
                               ev_d_skip[e], ev_gnorm_w[e], ev_rpb[e], ev_w_out[e])
        else:
            h = h + conv_module(rmsnorm(h, od_norm_w[e]), od_w_in[e], od_dw_w[e], od_dw_b[e],
                                od_ln_w[e], od_ln_b[e], od_w_out[e])
        h = h + per_layer_embedding(h, p[i], ple_norm_w[i], ple_w_gate[i], ple_w_proj[i])
    return rmsnorm(h, final_norm_w)
```

```python
import functools

import jax
import jax.numpy as jnp
from jax import lax
from jax.experimental import pallas as pl
from jax.experimental.pallas import tpu as pltpu

F32 = jnp.float32
BF16 = jnp.bfloat16

EPS = 1e-6
GRID_W = 64
SSM_STATE = 128
SSM_CHUNK = 128
SSM_HALO = 16
CONF_HALO = 16
NEG_BIG = -1e30
BF16_ROWS = 16

V7X_VMEM_LIMIT = 56 * 1024 * 1024


def _cparams(sem):
    return pltpu.CompilerParams(dimension_semantics=sem, vmem_limit_bytes=V7X_VMEM_LIMIT)


def _sigmoid(x):
    return 1.0 / (1.0 + jnp.exp(-x))


def _silu(x):
    return x * _sigmoid(x)


def _rms_rows(x, w):
    ms = jnp.mean(x * x, axis=-1, keepdims=True)
    return x * lax.rsqrt(ms + EPS) * w


def _norm_matmul_kernel(h_ref, nw_ref, *rest, glu):
    if glu:
        wa_ref, wb_ref, o_ref, xn_ref = rest
    else:
        wa_ref, o_ref, xn_ref = rest

    @pl.when(pl.program_id(1) == 0)
    def _():
        xn_ref[...] = _rms_rows(h_ref[...], nw_ref[...]).astype(BF16)

    xn = xn_ref[...]
    acc = jnp.dot(xn, wa_ref[...], preferred_element_type=F32)
    if glu:
        gate = jnp.dot(xn, wb_ref[...], preferred_element_type=F32)
        acc = acc * _sigmoid(gate)
    o_ref[...] = acc.astype(o_ref.dtype)


def _norm_matmul(h, norm_w, w, out_dtype, *, w_gate=None, tm=1024, tn=512):
    t, d = h.shape
    n = w.shape[1]
    tm = min(tm, t)
    tn = min(tn, n)
    assert t % tm == 0 and n % tn == 0
    glu = w_gate is not None
    w_spec = pl.BlockSpec((d, tn), lambda i, j: (0, j))
    in_specs = [pl.BlockSpec((tm, d), lambda i, j: (i, 0)),
                pl.BlockSpec((1, d), lambda i, j: (0, 0)),
                w_spec] + ([w_spec] if glu else [])
    args = (h, norm_w.reshape(1, d), w) + ((w_gate,) if glu else ())
    return pl.pallas_call(
        functools.partial(_norm_matmul_kernel, glu=glu),
        grid=(t // tm, n // tn),
        in_specs=in_specs,
        out_specs=pl.BlockSpec((tm, tn), lambda i, j: (i, j)),
        out_shape=jax.ShapeDtypeStruct((t, n), out_dtype),
        scratch_shapes=[pltpu.VMEM((tm, d), BF16)],
        compiler_params=_cparams(("parallel", "arbitrary")),
        name="norm_matmul_glu" if glu else "norm_matmul",
    )(*args)


def _proj_residual_kernel(*refs, n_x):
    h_ref = refs[0]
    x_refs = refs[1:1 + n_x]
    w_refs = refs[1 + n_x:1 + 2 * n_x]
    o_ref = refs[1 + 2 * n_x]
    acc = h_ref[...]
    for x_ref, w_ref in zip(x_refs, w_refs):
        acc = acc + jnp.dot(x_ref[...], w_ref[...], preferred_element_type=F32)
    o_ref[...] = acc


def _proj_residual(h, xs, ws, *, tm=512, tn=512):
    t, n = h.shape
    tm = min(tm, t)
    tn = min(tn, n)
    assert t % tm == 0 and n % tn == 0
    in_specs = [pl.BlockSpec((tm, tn), lambda i, j: (i, j))]
    in_specs += [pl.BlockSpec((tm, x.shape[1]), lambda i, j: (i, 0)) for x in xs]
    in_specs += [pl.BlockSpec((w.shape[0], tn), lambda i, j: (0, j)) for w in ws]
    return pl.pallas_call(
        functools.partial(_proj_residual_kernel, n_x=len(xs)),
        grid=(t // tm, n // tn),
        in_specs=in_specs,
        out_specs=pl.BlockSpec((tm, tn), lambda i, j: (i, j)),
        out_shape=jax.ShapeDtypeStruct((t, n), F32),
        compiler_params=_cparams(("parallel", "arbitrary")),
        name="proj_residual",
    )(h, *xs, *ws)


def _ple_kernel(hrow_ref, htile_ref, nw_ref, wg_ref, p_ref, wp_ref, o_ref, xn_ref):
    @pl.when(pl.program_id(1) == 0)
    def _():
        xn_ref[...] = _rms_rows(hrow_ref[...], nw_ref[...]).astype(BF16)

    gate = _sigmoid(jnp.dot(xn_ref[...], wg_ref[...], preferred_element_type=F32))
    proj = jnp.dot(p_ref[...].astype(BF16), wp_ref[...], preferred_element_type=F32)
    o_ref[...] = htile_ref[...] + gate * proj


def _ple(h, norm_w, w_gate, p, w_proj, *, tm=1024, tn=512):
    t, d = h.shape
    e = p.shape[1]
    tm = min(tm, t)
    tn = min(tn, d)
    assert t % tm == 0 and d % tn == 0
    return pl.pallas_call(
        _ple_kernel,
        grid=(t // tm, d // tn),
        in_specs=[pl.BlockSpec((tm, d), lambda i, j: (i, 0)),
                  pl.BlockSpec((tm, tn), lambda i, j: (i, j)),
                  pl.BlockSpec((1, d), lambda i, j: (0, 0)),
                  pl.BlockSpec((d, tn), lambda i, j: (0, j)),
                  pl.BlockSpec((tm, e), lambda i, j: (i, 0)),
                  pl.BlockSpec((e, tn), lambda i, j: (0, j))],
        out_specs=pl.BlockSpec((tm, tn), lambda i, j: (i, j)),
        out_shape=jax.ShapeDtypeStruct((t, d), F32),
        scratch_shapes=[pltpu.VMEM((tm, d), BF16)],
        compiler_params=_cparams(("parallel", "arbitrary")),
        name="per_layer_embedding",
    )(h, h, norm_w.reshape(1, d), w_gate, p, w_proj)


def _final_norm_kernel(h_ref, w_ref, o_ref):
    o_ref[...] = _rms_rows(h_ref[...], w_ref[...])


def _final_norm(h, w, *, tm=512):
    t, d = h.shape
    tm = min(tm, t)
    return pl.pallas_call(
        _final_norm_kernel,
        grid=(t // tm,),
        in_specs=[pl.BlockSpec((tm, d), lambda i: (i, 0)),
                  pl.BlockSpec((1, d), lambda i: (0, 0))],
        out_specs=pl.BlockSpec((tm, d), lambda i: (i, 0)),
        out_shape=jax.ShapeDtypeStruct((t, d), F32),
        compiler_params=_cparams(("parallel",)),
        name="final_rmsnorm",
    )(h, w.reshape(1, d))


def _split_dot(v, e):
    hi = v.astype(BF16)
    lo = (v - hi.astype(F32)).astype(BF16)
    return (jnp.dot(hi, e, preferred_element_type=F32)
            + jnp.dot(lo, e, preferred_element_type=F32))


def _ssd_kernel(*refs, rev, final, n_chunks, d, heads, groups, conv_k):
    (z_ref, x_ref, bc_ref, xp_ref, bcp_ref, xn_ref, bcn_ref, dt_ref) = refs[:8]
    pos = 8
    if final:
        yother_ref = refs[pos]
        pos += 1
    (cw_ref, cb_ref, dtb_ref, alog_ref, dskip_ref, gw_ref) = refs[pos:pos + 6]
    o_ref, xpad_ref, state_ref = refs[pos + 6:]

    q = SSM_CHUNK
    n = SSM_STATE
    hpg = heads // groups
    gw = d // groups
    p_dim = d // heads
    c = pl.program_id(1)
    ci = (n_chunks - 1 - c) if rev else c

    @pl.when(c == 0)
    def _():
        state_ref[...] = jnp.zeros_like(state_ref)

    has_prev = (ci > 0).astype(F32)
    has_next = (ci < n_chunks - 1).astype(F32)
    xpad_ref[0:SSM_HALO, 0:d] = xp_ref[0].astype(F32) * has_prev
    xpad_ref[0:SSM_HALO, d:2 * d] = bcp_ref[0].astype(F32) * has_prev
    xpad_ref[SSM_HALO:SSM_HALO + q, 0:d] = x_ref[0].astype(F32)
    xpad_ref[SSM_HALO:SSM_HALO + q, d:2 * d] = bc_ref[0].astype(F32)
    xpad_ref[SSM_HALO + q:, 0:d] = xn_ref[0].astype(F32) * has_next
    xpad_ref[SSM_HALO + q:, d:2 * d] = bcn_ref[0].astype(F32) * has_next

    def conv_silu(lo, width):
        acc = jnp.broadcast_to(cb_ref[:, lo:lo + width], (q, width))
        for k in range(conv_k):
            r0 = SSM_HALO - conv_k // 2 + k
            acc = acc + cw_ref[k:k + 1, lo:lo + width] * xpad_ref[r0:r0 + q, lo:lo + width]
        return _silu(acc)

    dsel = heads if rev else 0
    dt_raw = dt_ref[0][:, dsel:dsel + heads] + dtb_ref[...]
    dt = jnp.maximum(dt_raw, 0.0) + jnp.log1p(jnp.exp(-jnp.abs(dt_raw)))
    a = -jnp.exp(alog_ref[...])
    da = dt * a

    row = lax.broadcasted_iota(jnp.int32, (q, q), 0)
    col = lax.broadcasted_iota(jnp.int32, (q, q), 1)
    causal = (col >= row) if rev else (col <= row)
    tri = causal.astype(F32)
    cs_col = jnp.dot(tri, da, preferred_element_type=F32, precision=lax.Precision.HIGHEST)
    eye_h = (lax.broadcasted_iota(jnp.int32, (heads, heads), 0)
             == lax.broadcasted_iota(jnp.int32, (heads, heads), 1)).astype(F32)
    nt = (((1,), (1,)), ((), ()))
    cs_row = lax.dot_general(eye_h, cs_col, nt, preferred_element_type=F32,
                             precision=lax.Precision.HIGHEST)
    dt_row = lax.dot_general(eye_h, dt, nt, preferred_element_type=F32,
                             precision=lax.Precision.HIGHEST)
    last = 0 if rev else q - 1
    cs_last = cs_col[last:last + 1, :]

    e_lane = lax.broadcasted_iota(jnp.int32, (heads, d), 1)
    e_lo = lax.broadcasted_iota(jnp.int32, (heads, d), 0) * p_dim
    expand = jnp.where((e_lane >= e_lo) & (e_lane < e_lo + p_dim), 1.0, 0.0).astype(BF16)
    ecs_exp = _split_dot(jnp.exp(cs_col), expand)
    dec_exp = _split_dot(jnp.exp(cs_last - cs_col) * dt, expand)

    lane = lax.broadcasted_iota(jnp.int32, (1, gw), 1)

    for g in range(groups):
        xg = conv_silu(g * gw, gw)
        bg = conv_silu(d + g * n, n).astype(BF16)
        cg = conv_silu(d + groups * n + g * n, n).astype(BF16)
        gsl = slice(g * gw, (g + 1) * gw)
        xg_b = xg.astype(BF16)

        scores = lax.dot_general(cg, bg, nt, preferred_element_type=F32)
        m_parts = []
        r_parts = []
        for hh in range(hpg):
            h = g * hpg + hh
            diff = cs_col[:, h:h + 1] - cs_row[h:h + 1, :]
            decay = jnp.exp(jnp.where(causal, diff, NEG_BIG))
            m_parts.append((scores * decay * dt_row[h:h + 1, :]).astype(BF16))
            in_head = (lane >= hh * p_dim) & (lane < (hh + 1) * p_dim)
            r_parts.append(jnp.where(in_head, xg_b, jnp.zeros_like(xg_b)))
        m_cat = jnp.concatenate(m_parts, axis=1)
        r_cat = jnp.concatenate(r_parts, axis=0)
        y = jnp.dot(m_cat, r_cat, preferred_element_type=F32)

        st = state_ref[g]
        y = y + jnp.dot(cg, st.astype(BF16), preferred_element_type=F32) * ecs_exp[:, gsl]
        contrib = lax.dot_general(bg, (xg * dec_exp[:, gsl]).astype(BF16),
                                  (((0,), (0,)), ((), ())), preferred_element_type=F32)
        state_ref[g] = st * ecs_exp[last:last + 1, gsl] + contrib

        if final:
            y = y + yother_ref[0][:, gsl] + dskip_ref[:, gsl] * xg
            y = y * _silu(z_ref[0][:, gsl].astype(F32))
            ms = jnp.mean(y * y, axis=-1, keepdims=True)
            y = y * lax.rsqrt(ms + EPS) * gw_ref[:, gsl]
        o_ref[0, :, gsl] = y.astype(o_ref.dtype)


def _ssd_pass(u, dt_raw, y_other, conv_w, conv_b, dt_bias, a_log, d_skip_exp, gnorm_w,
              *, rev, heads, groups, d):
    b, s, _ = u.shape
    q = SSM_CHUNK
    n_chunks = s // q
    final = y_other is not None
    hb = q // SSM_HALO
    n_hblk = s // SSM_HALO

    def cidx(c):
        return (n_chunks - 1 - c) if rev else c

    def main(colblk):
        return pl.BlockSpec((1, q, d), lambda bi, c: (bi, cidx(c), colblk))

    def prev(colblk):
        return pl.BlockSpec((1, SSM_HALO, d),
                            lambda bi, c: (bi, jnp.maximum(cidx(c) * hb - 1, 0), colblk))

    def nxt(colblk):
        return pl.BlockSpec((1, SSM_HALO, d),
                            lambda bi, c: (bi, jnp.minimum(cidx(c) * hb + hb, n_hblk - 1), colblk))

    def whole(arr):
        return pl.BlockSpec(arr.shape, lambda bi, c: (0,) * arr.ndim)

    dtw = dt_raw.shape[-1]
    params = (conv_w, conv_b.reshape(1, -1), dt_bias.reshape(1, -1), a_log.reshape(1, -1),
              d_skip_exp.reshape(1, -1), gnorm_w.reshape(1, -1))
    in_specs = [main(0), main(1), main(2), prev(1), prev(2), nxt(1), nxt(2),
                pl.BlockSpec((1, q, dtw), lambda bi, c: (bi, cidx(c), 0))]
    args = [u, u, u, u, u, u, u, dt_raw]
    if final:
        in_specs.append(main(0))
        args.append(y_other)
    in_specs += [whole(a) for a in params]
    args += list(params)
    out_dtype = BF16 if final else F32
    return pl.pallas_call(
        functools.partial(_ssd_kernel, rev=rev, final=final, n_chunks=n_chunks, d=d,
                          heads=heads, groups=groups, conv_k=conv_w.shape[0]),
        grid=(b, n_chunks),
        in_specs=in_specs,
        out_specs=main(0),
        out_shape=jax.ShapeDtypeStruct((b, s, d), out_dtype),
        scratch_shapes=[pltpu.VMEM((q + 2 * SSM_HALO, 2 * d), F32),
                        pltpu.VMEM((groups, SSM_STATE, d // groups), F32)],
        compiler_params=_cparams(("parallel", "arbitrary")),
        name="ssd_final" if final else "ssd_first",
    )(*args)


def _na_kernel(q_ref, k_ref, v_ref, g_ref, bias_ref, o_ref, *, rows, kr, scale):
    w = GRID_W

    def body(r, carry):
        rs = jnp.clip(r - kr // 2, 0, rows - kr)
        q0 = pl.multiple_of(r * w, w)
        k0 = pl.multiple_of(rs * w, w)
        qr = q_ref[0, pl.ds(q0, w), :]
        kw = k_ref[0, pl.ds(k0, kr * w), :]
        vw = v_ref[0, pl.ds(k0, kr * w), :]
        s = lax.dot_general(qr, kw, (((1,), (1,)), ((), ())), preferred_element_type=F32)
        s = s * scale + bias_ref[0, rs - r + kr - 1]
        m = jnp.max(s, axis=-1, keepdims=True)
        e = jnp.exp(s - m)
        l = jnp.sum(e, axis=-1, keepdims=True)
        o = jnp.dot(e.astype(BF16), vw, preferred_element_type=F32) / l
        o = o * _silu(g_ref[0, pl.ds(q0, w), :].astype(F32))
        o_ref[0, pl.ds(q0, w), :] = o.astype(o_ref.dtype)
        return carry

    lax.fori_loop(0, rows, body, 0)


def _na_bias_table(rpb, kr):
    nh, n_ro, n_co = rpb.shape
    win_rows = (n_ro + 1) // 2
    win_cols = (n_co + 1) // 2
    cols = jnp.arange(GRID_W)
    col_start = jnp.clip(cols - win_cols // 2, 0, GRID_W - win_cols)
    col_valid = (cols[None, :] >= col_start[:, None]) & (cols[None, :] < col_start[:, None] + win_cols)
    col_off = jnp.clip(cols[None, :] - cols[:, None], -(win_cols - 1), win_cols - 1) + win_cols - 1
    row_off = (jnp.arange(kr)[:, None] - (kr - 1)) + jnp.arange(kr)[None, :] + win_rows - 1
    row_off = jnp.clip(row_off, 0, n_ro - 1)
    t = rpb.astype(F32)[:, row_off]
    t = t[:, :, :, col_off]
    t = jnp.where(col_valid[None, None, None], t, NEG_BIG)
    t = jnp.transpose(t, (0, 1, 3, 2, 4))
    return t.reshape(nh, kr, GRID_W, kr * GRID_W)


def _na_attention(u, rpb, *, d, col0, heads):
    b, s, _ = u.shape
    hd = d // heads
    rows = s // GRID_W
    win_rows = (rpb.shape[1] + 1) // 2
    kr = min(win_rows, rows)
    bias = _na_bias_table(rpb, kr)
    base = col0 // hd

    def sec(i):
        return pl.BlockSpec((1, s, hd), lambda bi, h: (bi, 0, base + i * heads + h))

    return pl.pallas_call(
        functools.partial(_na_kernel, rows=rows, kr=kr, scale=hd ** -0.5),
        grid=(b, heads),
        in_specs=[sec(0), sec(1), sec(2), sec(3),
                  pl.BlockSpec((1, kr, GRID_W, kr * GRID_W), lambda bi, h: (h, 0, 0, 0))],
        out_specs=pl.BlockSpec((1, s, hd), lambda bi, h: (bi, 0, h)),
        out_shape=jax.ShapeDtypeStruct((b, s, d), BF16),
        compiler_params=_cparams(("parallel", "arbitrary")),
        name="neighbourhood_attention",
    )(u, u, u, u, bias)


def _conformer_kernel(v_ref, vp_ref, vn_ref, g_ref, w_ref, b_ref, lnw_ref, lnb_ref,
                      o_ref, xpad_ref, xs_ref, conv_ref, *, n_blocks, width, rows, sub, cw):
    ci = pl.program_id(1)
    hl = CONF_HALO
    dc = v_ref.shape[-1]
    has_prev = (ci > 0).astype(F32)
    has_next = (ci < n_blocks - 1).astype(F32)
    xpad_ref[0:hl, :] = vp_ref[0].astype(F32) * has_prev
    xpad_ref[hl:hl + rows, :] = v_ref[0].astype(F32)
    xpad_ref[hl + rows:, :] = vn_ref[0].astype(F32) * has_next
    half = width // 2
    span = rows + 2 * hl - 8

    for cb in range(dc // cw):
        csl = slice(cb * cw, (cb + 1) * cw)
        for s in range(8):
            xs_ref[s, 0:span, :] = xpad_ref[s:s + span, csl]

        def conv_body(rb, carry):
            r0 = pl.multiple_of(rb * sub, sub)
            acc = jnp.broadcast_to(b_ref[:, csl], (sub, cw))
            for k in range(width):
                a, s = divmod(hl - half + k, 8)
                acc = acc + w_ref[k:k + 1, csl] * xs_ref[s, pl.ds(r0 + 8 * a, sub), :]
            conv_ref[pl.ds(r0, sub), csl] = acc
            return carry

        lax.fori_loop(0, rows // sub, conv_body, 0)

    def norm_body(rb, carry):
        r0 = pl.multiple_of(rb * BF16_ROWS, BF16_ROWS)
        acc = conv_ref[pl.ds(r0, BF16_ROWS), :]
        mu = jnp.mean(acc, axis=-1, keepdims=True)
        xc = acc - mu
        var = jnp.mean(xc * xc, axis=-1, keepdims=True)
        y = xc * lax.rsqrt(var + EPS) * lnw_ref[...] + lnb_ref[...]
        y = _silu(y) * _silu(g_ref[0, pl.ds(r0, BF16_ROWS), :].astype(F32))
        o_ref[0, pl.ds(r0, BF16_ROWS), :] = y.astype(o_ref.dtype)
        return carry

    lax.fori_loop(0, rows // BF16_ROWS, norm_body, 0)


def _conformer_mix(v, g_src, g_colblk, dw_w, dw_b, ln_w, ln_b, *, rows=256, sub=64, cw=256):
    b, s, dc = v.shape
    rows = min(rows, s)
    n_blocks = s // rows
    hb = rows // CONF_HALO
    n_hblk = s // CONF_HALO
    width = dw_w.shape[0]
    assert width // 2 < CONF_HALO

    def whole(arr):
        return pl.BlockSpec(arr.shape, lambda bi, c: (0,) * arr.ndim)

    params = (dw_w, dw_b.reshape(1, dc), ln_w.reshape(1, dc), ln_b.reshape(1, dc))
    return pl.pallas_call(
        functools.partial(_conformer_kernel, n_blocks=n_blocks, width=width, rows=rows,
                          sub=min(sub, rows), cw=cw),
        grid=(b, n_blocks),
        in_specs=[pl.BlockSpec((1, rows, dc), lambda bi, c: (bi, c, 0)),
                  pl.BlockSpec((1, CONF_HALO, dc), lambda bi, c: (bi, jnp.maximum(c * hb - 1, 0), 0)),
                  pl.BlockSpec((1, CONF_HALO, dc),
                               lambda bi, c: (bi, jnp.minimum(c * hb + hb, n_hblk - 1), 0)),
                  pl.BlockSpec((1, rows, dc), lambda bi, c: (bi, c, g_colblk))]
                 + [whole(a) for a in params],
        out_specs=pl.BlockSpec((1, rows, dc), lambda bi, c: (bi, c, 0)),
        out_shape=jax.ShapeDtypeStruct((b, s, dc), BF16),
        scratch_shapes=[pltpu.VMEM((rows + 2 * CONF_HALO, dc), F32),
                        pltpu.VMEM((8, rows + 2 * CONF_HALO, cw), F32),
                        pltpu.VMEM((rows, dc), F32)],
        compiler_params=_cparams(("parallel", "arbitrary")),
        name="conformer_conv_ln",
    )(v, v, v, g_src, *params)


def _even_layer(h, b, s, norm_w, w_in, conv_w, conv_b, dt_bias_f, dt_bias_b, a_log_f, a_log_b,
                d_skip, gnorm_w, rpb, w_out):
    t, d = h.shape
    heads = d_skip.shape[0]
    conv_ch = conv_w.shape[1]
    groups = (conv_ch - d) // (2 * SSM_STATE)
    assert conv_ch == 2 * d, "SSD conv channels are fetched as two d-wide column blocks"
    na_heads = rpb.shape[0]
    n_dt = 2 * heads
    c_dt = d + conv_ch
    c_q = c_dt + n_dt

    w_main = jnp.concatenate([w_in[:, :c_dt], w_in[:, c_q:]], axis=1).astype(BF16)
    dt_pad = (-n_dt) % 128
    w_dt = jnp.pad(w_in[:, c_dt:c_q], ((0, 0), (0, dt_pad))).astype(BF16)

    u = _norm_matmul(h, norm_w, w_main, BF16).reshape(b, s, -1)
    dt_raw = _norm_matmul(h, norm_w, w_dt, F32, tn=128).reshape(b, s, -1)

    d_skip_exp = jnp.repeat(d_skip, d // heads)
    common = dict(heads=heads, groups=groups, d=d)
    y_b = _ssd_pass(u, dt_raw, None, conv_w, conv_b, dt_bias_b, a_log_b, d_skip_exp, gnorm_w,
                    rev=True, **common)
    y_ssd = _ssd_pass(u, dt_raw, y_b, conv_w, conv_b, dt_bias_f, a_log_f, d_skip_exp, gnorm_w,
                      rev=False, **common)
    y_na = _na_attention(u, rpb, d=d, col0=c_dt, heads=na_heads)

    w_out_b = w_out.astype(BF16)
    return _proj_residual(h, [y_ssd.reshape(t, d), y_na.reshape(t, d)], [w_out_b[:d], w_out_b[d:]])


def _odd_layer(h, b, s, norm_w, w_in, dw_w, dw_b, ln_w, ln_b, w_out):
    t, d = h.shape
    dc = dw_w.shape[1]
    w_in_b = w_in.astype(BF16)
    v = _norm_matmul(h, norm_w, w_in_b[:, :dc], BF16, w_gate=w_in_b[:, dc:2 * dc])
    g = _norm_matmul(h, norm_w, w_in_b[:, 2 * dc:], BF16)
    y = _conformer_mix(v.reshape(b, s, dc), g.reshape(b, s, dc), 0, dw_w, dw_b, ln_w, ln_b)
    return _proj_residual(h, [y.reshape(t, dc)], [w_out.astype(BF16)])


def kernel(x, p, ev_norm_w, ev_w_in, ev_conv_w, ev_conv_b, ev_dt_bias_f, ev_dt_bias_b, ev_a_log_f, ev_a_log_b, ev_d_skip, ev_gnorm_w, ev_rpb, ev_w_out, od_norm_w, od_w_in, od_dw_w, od_dw_b, od_ln_w, od_ln_b, od_w_out, ple_norm_w, ple_w_gate, ple_w_proj, final_norm_w):
    b, s, d = x.shape
    depth = p.shape[0]
    h = x.reshape(b * s, d)
    for i in range(depth):
        e = i // 2
        if i % 2 == 0:
            h = _even_layer(h, b, s, ev_norm_w[e], ev_w_in[e], ev_conv_w[e], ev_conv_b[e],
                            ev_dt_bias_f[e], ev_dt_bias_b[e], ev_a_log_f[e], ev_a_log_b[e],
                            ev_d_skip[e], ev_gnorm_w[e], ev_rpb[e], ev_w_out[e])
        else:
            h = _odd_layer(h, b, s, od_norm_w[e], od_w_in[e], od_dw_w[e], od_dw_b[e],
                           od_ln_w[e], od_ln_b[e], od_w_out[e])
        h = _ple(h, ple_norm_w[i], ple_w_gate[i].astype(BF16), p[i].reshape(b * s, -1),
                 ple_w_proj[i].astype(BF16))
    return _final_norm(h, final_norm_w).reshape(b, s, d)
```

```python
import functools

import jax
import jax.numpy as jnp
from jax import lax
from jax.experimental import pallas as pl
from jax.experimental.pallas import tpu as pltpu

F32 = jnp.float32
BF16 = jnp.bfloat16

EPS = 1e-6
LANES = 128
SUBLANES = 8
GRID_W = 64
SSM_STATE = 128
SSM_CHUNK = 128
SSM_HALO = 16
CONF_HALO = 16
CONF_ACC = 16
NEG_BIG = -1e30
BF16_ROWS = 16
NA_ROW_UNROLL = 8

V7X_VMEM_LIMIT = 56 * 1024 * 1024


def _cparams(sem):
    return pltpu.CompilerParams(dimension_semantics=sem, vmem_limit_bytes=V7X_VMEM_LIMIT)


def _sigmoid(x):
    return 1.0 / (1.0 + jnp.exp(-x))


def _silu(x):
    return x * _sigmoid(x)


def _rms_rows(x, w):
    ms = jnp.mean(x * x, axis=-1, keepdims=True)
    return x * lax.rsqrt(ms + EPS) * w


def _norm_matmul_kernel(h_ref, nw_ref, w_ref, o_ref, xn_ref):
    @pl.when(pl.program_id(1) == 0)
    def _():
        xn_ref[...] = _rms_rows(h_ref[...], nw_ref[...]).astype(BF16)

    o_ref[...] = jnp.dot(xn_ref[...], w_ref[...], preferred_element_type=F32).astype(o_ref.dtype)


def _norm_matmul(h, norm_w, w, out_dtype, *, tm=1024, tn=512):
    t, d = h.shape
    n = w.shape[1]
    tm = min(tm, t)
    tn = min(tn, n)
    assert t % tm == 0 and n % tn == 0
    return pl.pallas_call(
        _norm_matmul_kernel,
        grid=(t // tm, n // tn),
        in_specs=[pl.BlockSpec((tm, d), lambda i, j: (i, 0)),
                  pl.BlockSpec((1, d), lambda i, j: (0, 0)),
                  pl.BlockSpec((d, tn), lambda i, j: (0, j))],
        out_specs=pl.BlockSpec((tm, tn), lambda i, j: (i, j)),
        out_shape=jax.ShapeDtypeStruct((t, n), out_dtype),
        scratch_shapes=[pltpu.VMEM((tm, d), BF16)],
        compiler_params=_cparams(("parallel", "arbitrary")),
        name="norm_matmul",
    )(h, norm_w.reshape(1, d), w)


def _conformer_in_kernel(h_ref, nw_ref, wa_ref, wb_ref, wg_ref, v_ref, g_ref, xn_ref):
    @pl.when(pl.program_id(1) == 0)
    def _():
        xn_ref[...] = _rms_rows(h_ref[...], nw_ref[...]).astype(BF16)

    xn = xn_ref[...]
    a = jnp.dot(xn, wa_ref[...], preferred_element_type=F32)
    a = a * _sigmoid(jnp.dot(xn, wb_ref[...], preferred_element_type=F32))
    for c in range(v_ref.shape[0]):
        v_ref[c] = a[:, c * LANES:(c + 1) * LANES].astype(v_ref.dtype)
    g_ref[...] = jnp.dot(xn, wg_ref[...], preferred_element_type=F32).astype(g_ref.dtype)


def _conformer_in(h, norm_w, w_a, w_b, w_g, *, tm=1024, tn=512):
    t, d = h.shape
    n = w_a.shape[1]
    tm = min(tm, t)
    tn = min(tn, n)
    assert t % tm == 0 and n % tn == 0 and tn % LANES == 0
    w_spec = pl.BlockSpec((d, tn), lambda i, j: (0, j))
    return pl.pallas_call(
        _conformer_in_kernel,
        grid=(t // tm, n // tn),
        in_specs=[pl.BlockSpec((tm, d), lambda i, j: (i, 0)),
                  pl.BlockSpec((1, d), lambda i, j: (0, 0)),
                  w_spec, w_spec, w_spec],
        out_specs=[pl.BlockSpec((tn // LANES, tm, LANES), lambda i, j: (j, i, 0)),
                   pl.BlockSpec((tm, tn), lambda i, j: (i, j))],
        out_shape=[jax.ShapeDtypeStruct((n // LANES, t, LANES), BF16),
                   jax.ShapeDtypeStruct((t, n), BF16)],
        scratch_shapes=[pltpu.VMEM((tm, d), BF16)],
        compiler_params=_cparams(("parallel", "arbitrary")),
        name="conformer_in_proj",
    )(h, norm_w.reshape(1, d), w_a, w_b, w_g)


def _proj_residual_kernel(*refs, n_x):
    h_ref = refs[0]
    x_refs = refs[1:1 + n_x]
    w_refs = refs[1 + n_x:1 + 2 * n_x]
    o_ref = refs[1 + 2 * n_x]
    acc = h_ref[...]
    for x_ref, w_ref in zip(x_refs, w_refs):
        acc = acc + jnp.dot(x_ref[...], w_ref[...], preferred_element_type=F32)
    o_ref[...] = acc


def _proj_residual(h, xs, ws, *, tm=512, tn=512):
    t, n = h.shape
    tm = min(tm, t)
    tn = min(tn, n)
    assert t % tm == 0 and n % tn == 0
    in_specs = [pl.BlockSpec((tm, tn), lambda i, j: (i, j))]
    in_specs += [pl.BlockSpec((tm, x.shape[1]), lambda i, j: (i, 0)) for x in xs]
    in_specs += [pl.BlockSpec((w.shape[0], tn), lambda i, j: (0, j)) for w in ws]
    return pl.pallas_call(
        functools.partial(_proj_residual_kernel, n_x=len(xs)),
        grid=(t // tm, n // tn),
        in_specs=in_specs,
        out_specs=pl.BlockSpec((tm, tn), lambda i, j: (i, j)),
        out_shape=jax.ShapeDtypeStruct((t, n), F32),
        compiler_params=_cparams(("parallel", "arbitrary")),
        name="proj_residual",
    )(h, *xs, *ws)


def _ple_kernel(hrow_ref, htile_ref, nw_ref, wg_ref, p_ref, wp_ref, o_ref, xn_ref):
    @pl.when(pl.program_id(1) == 0)
    def _():
        xn_ref[...] = _rms_rows(hrow_ref[...], nw_ref[...]).astype(BF16)

    gate = _sigmoid(jnp.dot(xn_ref[...], wg_ref[...], preferred_element_type=F32))
    proj = jnp.dot(p_ref[...].astype(BF16), wp_ref[...], preferred_element_type=F32)
    o_ref[...] = htile_ref[...] + gate * proj


def _ple(h, norm_w, w_gate, p, w_proj, *, tm=1024, tn=512):
    t, d = h.shape
    e = p.shape[1]
    tm = min(tm, t)
    tn = min(tn, d)
    assert t % tm == 0 and d % tn == 0
    return pl.pallas_call(
        _ple_kernel,
        grid=(t // tm, d // tn),
        in_specs=[pl.BlockSpec((tm, d), lambda i, j: (i, 0)),
                  pl.BlockSpec((tm, tn), lambda i, j: (i, j)),
                  pl.BlockSpec((1, d), lambda i, j: (0, 0)),
                  pl.BlockSpec((d, tn), lambda i, j: (0, j)),
                  pl.BlockSpec((tm, e), lambda i, j: (i, 0)),
                  pl.BlockSpec((e, tn), lambda i, j: (0, j))],
        out_specs=pl.BlockSpec((tm, tn), lambda i, j: (i, j)),
        out_shape=jax.ShapeDtypeStruct((t, d), F32),
        scratch_shapes=[pltpu.VMEM((tm, d), BF16)],
        compiler_params=_cparams(("parallel", "arbitrary")),
        name="per_layer_embedding",
    )(h, h, norm_w.reshape(1, d), w_gate, p, w_proj)


def _final_norm_kernel(h_ref, w_ref, o_ref):
    o_ref[...] = _rms_rows(h_ref[...], w_ref[...])


def _final_norm(h, w, *, tm=512):
    t, d = h.shape
    tm = min(tm, t)
    return pl.pallas_call(
        _final_norm_kernel,
        grid=(t // tm,),
        in_specs=[pl.BlockSpec((tm, d), lambda i: (i, 0)),
                  pl.BlockSpec((1, d), lambda i: (0, 0))],
        out_specs=pl.BlockSpec((tm, d), lambda i: (i, 0)),
        out_shape=jax.ShapeDtypeStruct((t, d), F32),
        compiler_params=_cparams(("parallel",)),
        name="final_rmsnorm",
    )(h, w.reshape(1, d))


def _split_dot(v, e):
    hi = v.astype(BF16)
    lo = (v - hi.astype(F32)).astype(BF16)
    return (jnp.dot(hi, e, preferred_element_type=F32)
            + jnp.dot(lo, e, preferred_element_type=F32))


def _ssd_kernel(*refs, rev, final, n_chunks, d, heads, groups, conv_k):
    if final:
        (z_ref, xbc_ref, dt_ref, yother_ref,
         dtb_ref, alog_ref, dskip_ref, gw_ref, o_ref, state_ref) = refs
    else:
        (x_ref, bc_ref, xp_ref, bcp_ref, xn_ref, bcn_ref, dt_ref,
         cw_ref, cb_ref, dtb_ref, alog_ref, o_ref, xbc_out_ref, xpad_ref, state_ref) = refs

    q = SSM_CHUNK
    n = SSM_STATE
    hpg = heads // groups
    gw = d // groups
    p_dim = d // heads
    c = pl.program_id(1)
    ci = (n_chunks - 1 - c) if rev else c

    @pl.when(c == 0)
    def _():
        state_ref[...] = jnp.zeros_like(state_ref)

    if final:
        def conv_silu(lo, width):
            return xbc_ref[0, :, lo:lo + width]
    else:
        has_prev = (ci > 0).astype(F32)
        has_next = (ci < n_chunks - 1).astype(F32)
        xpad_ref[0:SSM_HALO, 0:d] = xp_ref[0].astype(F32) * has_prev
        xpad_ref[0:SSM_HALO, d:2 * d] = bcp_ref[0].astype(F32) * has_prev
        xpad_ref[SSM_HALO:SSM_HALO + q, 0:d] = x_ref[0].astype(F32)
        xpad_ref[SSM_HALO:SSM_HALO + q, d:2 * d] = bc_ref[0].astype(F32)
        xpad_ref[SSM_HALO + q:, 0:d] = xn_ref[0].astype(F32) * has_next
        xpad_ref[SSM_HALO + q:, d:2 * d] = bcn_ref[0].astype(F32) * has_next

        def conv_silu(lo, width):
            acc = jnp.broadcast_to(cb_ref[:, lo:lo + width], (q, width))
            for k in range(conv_k):
                r0 = SSM_HALO - conv_k // 2 + k
                acc = acc + cw_ref[k:k + 1, lo:lo + width] * xpad_ref[r0:r0 + q, lo:lo + width]
            act = _silu(acc).astype(BF16)
            xbc_out_ref[0, :, lo:lo + width] = act
            return act

    dsel = heads if rev else 0
    dt_raw = dt_ref[0][:, dsel:dsel + heads] + dtb_ref[...]
    dt = jnp.maximum(dt_raw, 0.0) + jnp.log1p(jnp.exp(-jnp.abs(dt_raw)))
    a = -jnp.exp(alog_ref[...])
    da = dt * a

    row = lax.broadcasted_iota(jnp.int32, (q, q), 0)
    col = lax.broadcasted_iota(jnp.int32, (q, q), 1)
    causal = (col >= row) if rev else (col <= row)
    tri = causal.astype(F32)
    cs_col = jnp.dot(tri, da, preferred_element_type=F32, precision=lax.Precision.HIGHEST)
    eye_h = (lax.broadcasted_iota(jnp.int32, (heads, heads), 0)
             == lax.broadcasted_iota(jnp.int32, (heads, heads), 1)).astype(F32)
    nt = (((1,), (1,)), ((), ()))
    cs_row = lax.dot_general(eye_h, cs_col, nt, preferred_element_type=F32,
                             precision=lax.Precision.HIGHEST)
    dt_row = lax.dot_general(eye_h, dt, nt, preferred_element_type=F32,
                             precision=lax.Precision.HIGHEST)
    last = 0 if rev else q - 1
    cs_last = cs_col[last:last + 1, :]

    e_lane = lax.broadcasted_iota(jnp.int32, (heads, d), 1)
    e_lo = lax.broadcasted_iota(jnp.int32, (heads, d), 0) * p_dim
    expand = jnp.where((e_lane >= e_lo) & (e_lane < e_lo + p_dim), 1.0, 0.0).astype(BF16)
    ecs_exp = _split_dot(jnp.exp(cs_col), expand)
    dec_exp = _split_dot(jnp.exp(cs_last - cs_col) * dt, expand)

    lane = lax.broadcasted_iota(jnp.int32, (1, gw), 1)

    for g in range(groups):
        xg_b = conv_silu(g * gw, gw)
        bg = conv_silu(d + g * n, n)
        cg = conv_silu(d + groups * n + g * n, n)
        gsl = slice(g * gw, (g + 1) * gw)
        xg = xg_b.astype(F32)

        scores = lax.dot_general(cg, bg, nt, preferred_element_type=F32)
        m_parts = []
        r_parts = []
        for hh in range(hpg):
            h = g * hpg + hh
            diff = cs_col[:, h:h + 1] - cs_row[h:h + 1, :]
            decay = jnp.exp(jnp.where(causal, diff, NEG_BIG))
            m_parts.append((scores * decay * dt_row[h:h + 1, :]).astype(BF16))
            in_head = (lane >= hh * p_dim) & (lane < (hh + 1) * p_dim)
            r_parts.append(jnp.where(in_head, xg_b, jnp.zeros_like(xg_b)))
        m_cat = jnp.concatenate(m_parts, axis=1)
        r_cat = jnp.concatenate(r_parts, axis=0)
        y = jnp.dot(m_cat, r_cat, preferred_element_type=F32)

        st = state_ref[g]
        y = y + jnp.dot(cg, st.astype(BF16), preferred_element_type=F32) * ecs_exp[:, gsl]
        contrib = lax.dot_general(bg, (xg * dec_exp[:, gsl]).astype(BF16),
                                  (((0,), (0,)), ((), ())), preferred_element_type=F32)
        state_ref[g] = st * ecs_exp[last:last + 1, gsl] + contrib

        if final:
            y = y + yother_ref[0][:, gsl] + dskip_ref[:, gsl] * xg
            y = y * _silu(z_ref[0][:, gsl].astype(F32))
            ms = jnp.mean(y * y, axis=-1, keepdims=True)
            y = y * lax.rsqrt(ms + EPS) * gw_ref[:, gsl]
        o_ref[0, :, gsl] = y.astype(o_ref.dtype)


def _ssd_specs(s, d, rev):
    q = SSM_CHUNK
    n_chunks = s // q
    hb = q // SSM_HALO
    n_hblk = s // SSM_HALO

    def cidx(c):
        return (n_chunks - 1 - c) if rev else c

    def main(colblk, width=d):
        return pl.BlockSpec((1, q, width), lambda bi, c: (bi, cidx(c), colblk))

    def prev(colblk):
        return pl.BlockSpec((1, SSM_HALO, d),
                            lambda bi, c: (bi, jnp.maximum(cidx(c) * hb - 1, 0), colblk))

    def nxt(colblk):
        return pl.BlockSpec((1, SSM_HALO, d),
                            lambda bi, c: (bi, jnp.minimum(cidx(c) * hb + hb, n_hblk - 1), colblk))

    def whole(arr):
        return pl.BlockSpec(arr.shape, lambda bi, c: (0,) * arr.ndim)

    return n_chunks, main, prev, nxt, whole


def _ssd_first(u, dt_raw, conv_w, conv_b, dt_bias, a_log, *, heads, groups, d):
    b, s, _ = u.shape
    n_chunks, main, prev, nxt, whole = _ssd_specs(s, d, rev=True)
    params = (conv_w, conv_b.reshape(1, -1), dt_bias.reshape(1, -1), a_log.reshape(1, -1))
    return pl.pallas_call(
        functools.partial(_ssd_kernel, rev=True, final=False, n_chunks=n_chunks, d=d,
                          heads=heads, groups=groups, conv_k=conv_w.shape[0]),
        grid=(b, n_chunks),
        in_specs=[main(1), main(2), prev(1), prev(2), nxt(1), nxt(2), main(0, dt_raw.shape[-1])]
                 + [whole(a) for a in params],
        out_specs=[main(0), main(0, 2 * d)],
        out_shape=[jax.ShapeDtypeStruct((b, s, d), F32),
                   jax.ShapeDtypeStruct((b, s, 2 * d), BF16)],
        scratch_shapes=[pltpu.VMEM((SSM_CHUNK + 2 * SSM_HALO, 2 * d), F32),
                        pltpu.VMEM((groups, SSM_STATE, d // groups), F32)],
        compiler_params=_cparams(("parallel", "arbitrary")),
        name="ssd_first",
    )(u, u, u, u, u, u, dt_raw, *params)


def _ssd_final(u, xbc, dt_raw, y_other, dt_bias, a_log, d_skip_exp, gnorm_w, *, heads, groups, d):
    b, s, _ = u.shape
    n_chunks, main, _, _, whole = _ssd_specs(s, d, rev=False)
    params = (dt_bias.reshape(1, -1), a_log.reshape(1, -1), d_skip_exp.reshape(1, -1),
              gnorm_w.reshape(1, -1))
    return pl.pallas_call(
        functools.partial(_ssd_kernel, rev=False, final=True, n_chunks=n_chunks, d=d,
                          heads=heads, groups=groups, conv_k=0),
        grid=(b, n_chunks),
        in_specs=[main(0), main(0, 2 * d), main(0, dt_raw.shape[-1]), main(0)]
                 + [whole(a) for a in params],
        out_specs=main(0),
        out_shape=jax.ShapeDtypeStruct((b, s, d), BF16),
        scratch_shapes=[pltpu.VMEM((groups, SSM_STATE, d // groups), F32)],
        compiler_params=_cparams(("parallel", "arbitrary")),
        name="ssd_final",
    )(u, xbc, dt_raw, y_other, *params)


def _na_kernel(q_ref, k_ref, v_ref, g_ref, bias_ref, o_ref, *, rows, kr):
    w = GRID_W

    def body(it, carry):
        rr = [it * NA_ROW_UNROLL + u for u in range(NA_ROW_UNROLL)]
        rs = [jnp.clip(r - kr // 2, 0, rows - kr) for r in rr]
        q0 = [pl.multiple_of(r * w, w) for r in rr]
        k0 = [pl.multiple_of(x * w, w) for x in rs]
        s = [lax.dot_general(q_ref[0, pl.ds(q0[u], w), :], k_ref[0, pl.ds(k0[u], kr * w), :],
                             (((1,), (1,)), ((), ())), preferred_element_type=F32)
             for u in range(NA_ROW_UNROLL)]
        s = [s[u] + bias_ref[0, rs[u] - rr[u] + kr - 1] for u in range(NA_ROW_UNROLL)]
        e = [jnp.exp(x - jnp.max(x, axis=-1, keepdims=True)) for x in s]
        l = [jnp.sum(x, axis=-1, keepdims=True) for x in e]
        o = [jnp.dot(e[u].astype(BF16), v_ref[0, pl.ds(k0[u], kr * w), :],
                     preferred_element_type=F32) for u in range(NA_ROW_UNROLL)]
        for u in range(NA_ROW_UNROLL):
            gate = _silu(g_ref[0, pl.ds(q0[u], w), :].astype(F32))
            o_ref[0, pl.ds(q0[u], w), :] = (o[u] / l[u] * gate).astype(o_ref.dtype)
        return carry

    lax.fori_loop(0, rows // NA_ROW_UNROLL, body, 0)


def _na_bias_table(rpb, kr):
    nh, n_ro, n_co = rpb.shape
    win_rows = (n_ro + 1) // 2
    win_cols = (n_co + 1) // 2
    cols = jnp.arange(GRID_W)
    col_start = jnp.clip(cols - win_cols // 2, 0, GRID_W - win_cols)
    col_valid = (cols[None, :] >= col_start[:, None]) & (cols[None, :] < col_start[:, None] + win_cols)
    col_off = jnp.clip(cols[None, :] - cols[:, None], -(win_cols - 1), win_cols - 1) + win_cols - 1
    row_off = (jnp.arange(kr)[:, None] - (kr - 1)) + jnp.arange(kr)[None, :] + win_rows - 1
    row_off = jnp.clip(row_off, 0, n_ro - 1)
    t = rpb.astype(F32)[:, row_off]
    t = t[:, :, :, col_off]
    t = jnp.where(col_valid[None, None, None], t, NEG_BIG)
    t = jnp.transpose(t, (0, 1, 3, 2, 4))
    return t.reshape(nh, kr, GRID_W, kr * GRID_W)


def _na_attention(u, rpb, *, d, col0, heads):
    b, s, _ = u.shape
    hd = d // heads
    rows = s // GRID_W
    win_rows = (rpb.shape[1] + 1) // 2
    kr = min(win_rows, rows)
    bias = _na_bias_table(rpb, kr)
    base = col0 // hd

    def sec(i):
        return pl.BlockSpec((1, s, hd), lambda bi, h: (bi, 0, base + i * heads + h))

    return pl.pallas_call(
        functools.partial(_na_kernel, rows=rows, kr=kr),
        grid=(b, heads),
        in_specs=[sec(0), sec(1), sec(2), sec(3),
                  pl.BlockSpec((1, kr, GRID_W, kr * GRID_W), lambda bi, h: (h, 0, 0, 0))],
        out_specs=pl.BlockSpec((1, s, hd), lambda bi, h: (bi, 0, h)),
        out_shape=jax.ShapeDtypeStruct((b, s, d), BF16),
        compiler_params=_cparams(("parallel", "arbitrary")),
        name="neighbourhood_attention",
    )(u, u, u, u, bias)


def _conformer_kernel(v_ref, vp_ref, vn_ref, g_ref, w_ref, b_ref, lnw_ref, lnb_ref,
                      o_ref, xs_ref, conv_ref, *, n_blocks, width, rows):
    ci = pl.program_id(1)
    hl = CONF_HALO
    n_strips = v_ref.shape[0]
    seg = rows // SUBLANES
    p_in = seg + 2 * hl + SUBLANES
    p_out = seg + SUBLANES
    half = width // 2
    has_prev = (ci > 0).astype(F32)
    has_next = (ci < n_blocks - 1).astype(F32)

    for j in range(SUBLANES):
        base = p_in * j
        lo = seg * j - hl
        hi = seg * j + seg + hl
        if lo < 0:
            xs_ref[:, base:base + hl, :] = vp_ref[...].astype(F32) * has_prev
            xs_ref[:, base + hl:base + hl + hi, :] = v_ref[:, 0:hi, :].astype(F32)
        elif hi > rows:
            xs_ref[:, base:base + rows - lo, :] = v_ref[:, lo:rows, :].astype(F32)
            xs_ref[:, base + rows - lo:base + seg + 2 * hl, :] = vn_ref[...].astype(F32) * has_next
        else:
            xs_ref[:, base:base + seg + 2 * hl, :] = v_ref[:, lo:hi, :].astype(F32)

    off = hl - half

    def strip_body(s, carry):
        wv = [jnp.broadcast_to(w_ref[k, s], (SUBLANES, LANES)) for k in range(width)]
        bv = jnp.broadcast_to(b_ref[s], (SUBLANES, LANES))
        for i0 in range(0, seg, CONF_ACC):
            acc = [bv] * CONF_ACC
            for m in range(i0 + off, i0 + CONF_ACC - 1 + off + width):
                xv = xs_ref[s, pl.ds(m, SUBLANES, stride=p_in), :]
                for i in range(i0, i0 + CONF_ACC):
                    k = m - i - off
                    if 0 <= k < width:
                        acc[i - i0] = acc[i - i0] + wv[k] * xv
            for i in range(i0, i0 + CONF_ACC):
                conv_ref[s, pl.ds(i, SUBLANES, stride=p_out), :] = acc[i - i0]
        return carry

    lax.fori_loop(0, n_strips, strip_body, 0)

    inv_dc = 1.0 / (n_strips * LANES)

    def norm_body(j, carry):
        for rb in range(seg // BF16_ROWS):
            r_in = pl.multiple_of(j * p_out + rb * BF16_ROWS, SUBLANES)
            t0 = pl.multiple_of(j * seg + rb * BF16_ROWS, BF16_ROWS)
            x = conv_ref[:, pl.ds(r_in, BF16_ROWS), :]
            mu = jnp.sum(jnp.sum(x, axis=0), axis=-1, keepdims=True) * inv_dc
            xc = x - mu[None]
            var = jnp.sum(jnp.sum(xc * xc, axis=0), axis=-1, keepdims=True) * inv_dc
            y = _silu(xc * lax.rsqrt(var + EPS)[None] * lnw_ref[...] + lnb_ref[...])
            y = jnp.concatenate([y[s] for s in range(n_strips)], axis=-1)
            y = y * _silu(g_ref[0, pl.ds(t0, BF16_ROWS), :].astype(F32))
            o_ref[0, pl.ds(t0, BF16_ROWS), :] = y.astype(o_ref.dtype)
        return carry

    lax.fori_loop(0, SUBLANES, norm_body, 0)


def _conformer_mix(v, g, dw_w, dw_b, ln_w, ln_b, *, b, rows=256):
    n_strips, t, _ = v.shape
    dc = n_strips * LANES
    s = t // b
    rows = min(rows, s)
    n_blocks = s // rows
    hb = rows // CONF_HALO
    n_hblk = s // CONF_HALO
    width = dw_w.shape[0]
    seg = rows // SUBLANES
    assert width // 2 <= CONF_HALO <= seg and seg % BF16_ROWS == 0 and seg % CONF_ACC == 0

    def whole(arr):
        return pl.BlockSpec(arr.shape, lambda bi, c: (0,) * arr.ndim)

    params = (dw_w.reshape(width, n_strips, 1, LANES), dw_b.reshape(n_strips, 1, LANES),
              ln_w.reshape(n_strips, 1, LANES), ln_b.reshape(n_strips, 1, LANES))
    return pl.pallas_call(
        functools.partial(_conformer_kernel, n_blocks=n_blocks, width=width, rows=rows),
        grid=(b, n_blocks),
        in_specs=[pl.BlockSpec((n_strips, rows, LANES), lambda bi, c: (0, bi * n_blocks + c, 0)),
                  pl.BlockSpec((n_strips, CONF_HALO, LANES),
                               lambda bi, c: (0, bi * n_hblk + jnp.maximum(c * hb - 1, 0), 0)),
                  pl.BlockSpec((n_strips, CONF_HALO, LANES),
                               lambda bi, c: (0, bi * n_hblk + jnp.minimum(c * hb + hb, n_hblk - 1), 0)),
                  pl.BlockSpec((1, rows, dc), lambda bi, c: (bi, c, 0))]
                 + [whole(a) for a in params],
        out_specs=pl.BlockSpec((1, rows, dc), lambda bi, c: (bi, c, 0)),
        out_shape=jax.ShapeDtypeStruct((b, s, dc), BF16),
        scratch_shapes=[pltpu.VMEM((n_strips, SUBLANES * (seg + 2 * CONF_HALO + SUBLANES), LANES), F32),
                        pltpu.VMEM((n_strips, SUBLANES * (seg + SUBLANES), LANES), F32)],
        compiler_params=_cparams(("parallel", "arbitrary")),
        name="conformer_conv_ln",
    )(v, v, v, g.reshape(b, s, dc), *params)


def _even_layer(h, b, s, norm_w, w_in, conv_w, conv_b, dt_bias_f, dt_bias_b, a_log_f, a_log_b,
                d_skip, gnorm_w, rpb, w_out):
    t, d = h.shape
    heads = d_skip.shape[0]
    conv_ch = conv_w.shape[1]
    groups = (conv_ch - d) // (2 * SSM_STATE)
    assert conv_ch == 2 * d, "SSD conv channels are fetched as two d-wide column blocks"
    na_heads = rpb.shape[0]
    n_dt = 2 * heads
    c_dt = d + conv_ch
    c_q = c_dt + n_dt

    q_scale = (d // na_heads) ** -0.5
    w_main = jnp.concatenate([w_in[:, :c_dt], w_in[:, c_q:c_q + d] * q_scale, w_in[:, c_q + d:]],
                             axis=1).astype(BF16)
    dt_pad = (-n_dt) % 128
    w_dt = jnp.pad(w_in[:, c_dt:c_q], ((0, 0), (0, dt_pad))).astype(BF16)

    u = _norm_matmul(h, norm_w, w_main, BF16).reshape(b, s, -1)
    dt_raw = _norm_matmul(h, norm_w, w_dt, F32, tn=128).reshape(b, s, -1)

    d_skip_exp = jnp.repeat(d_skip, d // heads)
    common = dict(heads=heads, groups=groups, d=d)
    y_b, xbc = _ssd_first(u, dt_raw, conv_w, conv_b, dt_bias_b, a_log_b, **common)
    y_ssd = _ssd_final(u, xbc, dt_raw, y_b, dt_bias_f, a_log_f, d_skip_exp, gnorm_w, **common)
    y_na = _na_attention(u, rpb, d=d, col0=c_dt, heads=na_heads)

    w_out_b = w_out.astype(BF16)
    return _proj_residual(h, [y_ssd.reshape(t, d), y_na.reshape(t, d)], [w_out_b[:d], w_out_b[d:]])


def _odd_layer(h, b, s, norm_w, w_in, dw_w, dw_b, ln_w, ln_b, w_out):
    t, d = h.shape
    dc = dw_w.shape[1]
    w_in_b = w_in.astype(BF16)
    v, g = _conformer_in(h, norm_w, w_in_b[:, :dc], w_in_b[:, dc:2 * dc], w_in_b[:, 2 * dc:])
    y = _conformer_mix(v, g, dw_w, dw_b, ln_w, ln_b, b=b)
    return _proj_residual(h, [y.reshape(t, dc)], [w_out.astype(BF16)])


def kernel(x, p, ev_norm_w, ev_w_in, ev_conv_w, ev_conv_b, ev_dt_bias_f, ev_dt_bias_b, ev_a_log_f, ev_a_log_b, ev_d_skip, ev_gnorm_w, ev_rpb, ev_w_out, od_norm_w, od_w_in, od_dw_w, od_dw_b, od_ln_w, od_ln_b, od_w_out, ple_norm_w, ple_w_gate, ple_w_proj, final_norm_w):
    b, s, d = x.shape
    depth = p.shape[0]
    h = x.reshape(b * s, d)
    for i in range(depth):
        e = i // 2
        if i % 2 == 0:
            h = _even_layer(h, b, s, ev_norm_w[e], ev_w_in[e], ev_conv_w[e], ev_conv_b[e],
                            ev_dt_bias_f[e], ev_dt_bias_b[e], ev_a_log_f[e], ev_a_log_b[e],
                            ev_d_skip[e], ev_gnorm_w[e], ev_rpb[e], ev_w_out[e])
        else:
            h = _odd_layer(h, b, s, od_norm_w[e], od_w_in[e], od_dw_w[e], od_dw_b[e],
                           od_ln_w[e], od_ln_b[e], od_w_out[e])
        h = _ple(h, ple_norm_w[i], ple_w_gate[i].astype(BF16), p[i].reshape(b * s, -1),
                 ple_w_proj[i].astype(BF16))
    return _final_norm(h, final_norm_w).reshape(b, s, d)
```

```python
import functools

import jax
import jax.numpy as jnp
from jax import lax
from jax.experimental import pallas as pl
from jax.experimental.pallas import tpu as pltpu

F32 = jnp.float32
BF16 = jnp.bfloat16

EPS = 1e-6
LANES = 128
SUBLANES = 8
GRID_W = 64
SSM_STATE = 128
SSM_CHUNK = 128
SSM_HALO = 16
SSM_CONV_COLS = 256
CONF_HALO = 16
CONF_ACC = 16
NEG_BIG = -1e30
BF16_ROWS = 16
NA_ROW_UNROLL = 8

V7X_VMEM_LIMIT = 56 * 1024 * 1024


def _cparams(sem):
    return pltpu.CompilerParams(dimension_semantics=sem, vmem_limit_bytes=V7X_VMEM_LIMIT)


def _sigmoid(x):
    return 1.0 / (1.0 + jnp.exp(-x))


def _silu(x):
    return x * _sigmoid(x)


def _rms_rows(x, w):
    ms = jnp.mean(x * x, axis=-1, keepdims=True)
    return x * lax.rsqrt(ms + EPS) * w


def _store_strips(o_ref, val):
    for c in range(o_ref.shape[0]):
        o_ref[c] = val[:, c * LANES:(c + 1) * LANES].astype(o_ref.dtype)


def _load_strips(x_ref):
    return jnp.concatenate([x_ref[c] for c in range(x_ref.shape[0])], axis=-1)


def _norm_matmul_kernel(h_ref, nw_ref, w_ref, o_ref, xn_ref, *, strips, silu_from):
    @pl.when(pl.program_id(1) == 0)
    def _():
        xn_ref[...] = _rms_rows(h_ref[...], nw_ref[...]).astype(BF16)

    acc = jnp.dot(xn_ref[...], w_ref[...], preferred_element_type=F32)
    if not strips:
        o_ref[...] = acc.astype(o_ref.dtype)
    elif silu_from is None:
        _store_strips(o_ref, acc)
    else:
        @pl.when(pl.program_id(1) < silu_from)
        def _():
            _store_strips(o_ref, acc)

        @pl.when(pl.program_id(1) >= silu_from)
        def _():
            _store_strips(o_ref, _silu(acc))


def _norm_matmul(h, norm_w, w, out_dtype, *, strips=False, silu_from_col=None, tm=1024, tn=512):
    t, d = h.shape
    n = w.shape[1]
    tm = min(tm, t)
    tn = min(tn, n)
    assert t % tm == 0 and n % tn == 0 and tn % LANES == 0
    silu_from = None
    if silu_from_col is not None:
        assert strips and silu_from_col % tn == 0
        silu_from = silu_from_col // tn
    if strips:
        out_spec = pl.BlockSpec((tn // LANES, tm, LANES), lambda i, j: (j, i, 0))
        out_shape = jax.ShapeDtypeStruct((n // LANES, t, LANES), out_dtype)
    else:
        out_spec = pl.BlockSpec((tm, tn), lambda i, j: (i, j))
        out_shape = jax.ShapeDtypeStruct((t, n), out_dtype)
    return pl.pallas_call(
        functools.partial(_norm_matmul_kernel, strips=strips, silu_from=silu_from),
        grid=(t // tm, n // tn),
        in_specs=[pl.BlockSpec((tm, d), lambda i, j: (i, 0)),
                  pl.BlockSpec((1, d), lambda i, j: (0, 0)),
                  pl.BlockSpec((d, tn), lambda i, j: (0, j))],
        out_specs=out_spec,
        out_shape=out_shape,
        scratch_shapes=[pltpu.VMEM((tm, d), BF16)],
        compiler_params=_cparams(("parallel", "arbitrary")),
        name="norm_matmul_strips" if strips else "norm_matmul",
    )(h, norm_w.reshape(1, d), w)


def _conformer_in_kernel(h_ref, nw_ref, wa_ref, wb_ref, wg_ref, v_ref, g_ref, xn_ref):
    @pl.when(pl.program_id(1) == 0)
    def _():
        xn_ref[...] = _rms_rows(h_ref[...], nw_ref[...]).astype(BF16)

    xn = xn_ref[...]
    a = jnp.dot(xn, wa_ref[...], preferred_element_type=F32)
    a = a * _sigmoid(jnp.dot(xn, wb_ref[...], preferred_element_type=F32))
    _store_strips(v_ref, a)
    g_ref[...] = _silu(jnp.dot(xn, wg_ref[...], preferred_element_type=F32)).astype(g_ref.dtype)


def _conformer_in(h, norm_w, w, *, tm=1024, tn=512):
    t, d = h.shape
    n = w.shape[1] // 3
    tm = min(tm, t)
    tn = min(tn, n)
    assert t % tm == 0 and n % tn == 0 and tn % LANES == 0
    nb = n // tn

    def w_spec(section):
        return pl.BlockSpec((d, tn), lambda i, j: (0, section * nb + j))

    return pl.pallas_call(
        _conformer_in_kernel,
        grid=(t // tm, n // tn),
        in_specs=[pl.BlockSpec((tm, d), lambda i, j: (i, 0)),
                  pl.BlockSpec((1, d), lambda i, j: (0, 0)),
                  w_spec(0), w_spec(1), w_spec(2)],
        out_specs=[pl.BlockSpec((tn // LANES, tm, LANES), lambda i, j: (j, i, 0)),
                   pl.BlockSpec((tm, tn), lambda i, j: (i, j))],
        out_shape=[jax.ShapeDtypeStruct((n // LANES, t, LANES), BF16),
                   jax.ShapeDtypeStruct((t, n), BF16)],
        scratch_shapes=[pltpu.VMEM((tm, d), BF16)],
        compiler_params=_cparams(("parallel", "arbitrary")),
        name="conformer_in_proj",
    )(h, norm_w.reshape(1, d), w, w, w)


def _proj_residual_kernel(*refs, n_x):
    h_ref = refs[0]
    x_refs = refs[1:1 + n_x]
    w_refs = refs[1 + n_x:1 + 2 * n_x]
    o_ref = refs[1 + 2 * n_x]
    acc = h_ref[...]
    for x_ref, w_ref in zip(x_refs, w_refs):
        x = _load_strips(x_ref) if len(x_ref.shape) == 3 else x_ref[...]
        acc = acc + jnp.dot(x, w_ref[...], preferred_element_type=F32)
    o_ref[...] = acc


def _proj_residual(h, xs, w, *, tm=1024, tn=512):
    t, n = h.shape
    tm = min(tm, t)
    tn = min(tn, n)
    assert t % tm == 0 and n % tn == 0
    k = w.shape[0] // len(xs)
    in_specs = [pl.BlockSpec((tm, tn), lambda i, j: (i, j))]
    for x in xs:
        if x.ndim == 3:
            assert x.shape[0] * LANES == k
            in_specs.append(pl.BlockSpec((x.shape[0], tm, LANES), lambda i, j: (0, i, 0)))
        else:
            assert x.shape[1] == k
            in_specs.append(pl.BlockSpec((tm, k), lambda i, j: (i, 0)))
    in_specs += [pl.BlockSpec((k, tn), lambda i, j, r=r: (r, j)) for r in range(len(xs))]
    return pl.pallas_call(
        functools.partial(_proj_residual_kernel, n_x=len(xs)),
        grid=(t // tm, n // tn),
        in_specs=in_specs,
        out_specs=pl.BlockSpec((tm, tn), lambda i, j: (i, j)),
        out_shape=jax.ShapeDtypeStruct((t, n), F32),
        compiler_params=_cparams(("parallel", "arbitrary")),
        name="proj_residual",
    )(h, *xs, *([w] * len(xs)))


def _ple_kernel(hrow_ref, htile_ref, nw_ref, wg_ref, p_ref, wp_ref, o_ref, xn_ref):
    @pl.when(pl.program_id(1) == 0)
    def _():
        xn_ref[...] = _rms_rows(hrow_ref[...], nw_ref[...]).astype(BF16)

    gate = _sigmoid(jnp.dot(xn_ref[...], wg_ref[...], preferred_element_type=F32))
    proj = jnp.dot(p_ref[...].astype(BF16), wp_ref[...], preferred_element_type=F32)
    o_ref[...] = htile_ref[...] + gate * proj


def _ple(h, norm_w, w_gate, p, w_proj, *, tm=1024, tn=512):
    t, d = h.shape
    e = p.shape[1]
    tm = min(tm, t)
    tn = min(tn, d)
    assert t % tm == 0 and d % tn == 0
    return pl.pallas_call(
        _ple_kernel,
        grid=(t // tm, d // tn),
        in_specs=[pl.BlockSpec((tm, d), lambda i, j: (i, 0)),
                  pl.BlockSpec((tm, tn), lambda i, j: (i, j)),
                  pl.BlockSpec((1, d), lambda i, j: (0, 0)),
                  pl.BlockSpec((d, tn), lambda i, j: (0, j)),
                  pl.BlockSpec((tm, e), lambda i, j: (i, 0)),
                  pl.BlockSpec((e, tn), lambda i, j: (0, j))],
        out_specs=pl.BlockSpec((tm, tn), lambda i, j: (i, j)),
        out_shape=jax.ShapeDtypeStruct((t, d), F32),
        scratch_shapes=[pltpu.VMEM((tm, d), BF16)],
        compiler_params=_cparams(("parallel", "arbitrary")),
        name="per_layer_embedding",
    )(h, h, norm_w.reshape(1, d), w_gate, p, w_proj)


def _final_norm_kernel(h_ref, w_ref, o_ref):
    o_ref[...] = _rms_rows(h_ref[...], w_ref[...])


def _final_norm(h, w, *, tm=512):
    t, d = h.shape
    tm = min(tm, t)
    return pl.pallas_call(
        _final_norm_kernel,
        grid=(t // tm,),
        in_specs=[pl.BlockSpec((tm, d), lambda i: (i, 0)),
                  pl.BlockSpec((1, d), lambda i: (0, 0))],
        out_specs=pl.BlockSpec((tm, d), lambda i: (i, 0)),
        out_shape=jax.ShapeDtypeStruct((t, d), F32),
        compiler_params=_cparams(("parallel",)),
        name="final_rmsnorm",
    )(h, w.reshape(1, d))


def _split_dot(v, e):
    hi = v.astype(BF16)
    lo = (v - hi.astype(F32)).astype(BF16)
    return (jnp.dot(hi, e, preferred_element_type=F32)
            + jnp.dot(lo, e, preferred_element_type=F32))


def _ssd_kernel(*refs, rev, final, n_chunks, d, heads, groups, conv_k):
    if final:
        (z_ref, xbc_ref, dt_ref, yother_ref,
         dtb_ref, alog_ref, dskip_ref, gw_ref, o_ref, state_ref) = refs
    else:
        (x_ref, bc_ref, xp_ref, bcp_ref, xn_ref, bcn_ref, dt_ref,
         cw_ref, cb_ref, dtb_ref, alog_ref, o_ref, xbc_out_ref, xpad_ref, state_ref) = refs

    q = SSM_CHUNK
    n = SSM_STATE
    hpg = heads // groups
    gw = d // groups
    p_dim = d // heads
    c = pl.program_id(1)
    ci = (n_chunks - 1 - c) if rev else c

    @pl.when(c == 0)
    def _():
        state_ref[...] = jnp.zeros_like(state_ref)

    if final:
        def conv_silu(lo, width):
            return xbc_ref[0, :, lo:lo + width]
    else:
        hl = SSM_HALO
        no_halo = jnp.zeros((hl, d), BF16)
        xpad_ref[0:hl, 0:d] = jnp.where(ci > 0, xp_ref[0], no_halo)
        xpad_ref[0:hl, d:2 * d] = jnp.where(ci > 0, bcp_ref[0], no_halo)
        xpad_ref[hl:hl + q, 0:d] = x_ref[0]
        xpad_ref[hl:hl + q, d:2 * d] = bc_ref[0]
        xpad_ref[hl + q:, 0:d] = jnp.where(ci < n_chunks - 1, xn_ref[0], no_halo)
        xpad_ref[hl + q:, d:2 * d] = jnp.where(ci < n_chunks - 1, bcn_ref[0], no_halo)

        half = conv_k // 2
        taps = [k for k in range(conv_k) if k != half]
        sel_row = lax.broadcasted_iota(jnp.int32, (q, q + 2 * hl), 0)
        sel_col = lax.broadcasted_iota(jnp.int32, (q, q + 2 * hl), 1)
        sel = jnp.concatenate([jnp.where(sel_col == sel_row + (hl + k - half), 1.0, 0.0).astype(BF16)
                               for k in taps], axis=0)
        for lo in range(0, 2 * d, SSM_CONV_COLS):
            csl = slice(lo, lo + SSM_CONV_COLS)
            xc = xpad_ref[:, csl]
            shifted = jnp.dot(sel, xc, preferred_element_type=F32)
            acc = cb_ref[:, csl] + cw_ref[half:half + 1, csl] * xc[hl:hl + q].astype(F32)
            for kk, k in enumerate(taps):
                acc = acc + cw_ref[k:k + 1, csl] * shifted[kk * q:(kk + 1) * q]
            xbc_out_ref[0, :, csl] = _silu(acc).astype(BF16)

        def conv_silu(lo, width):
            return xbc_out_ref[0, :, lo:lo + width]

    dsel = heads if rev else 0
    dt_raw = dt_ref[0][:, dsel:dsel + heads] + dtb_ref[...]
    dt = jnp.maximum(dt_raw, 0.0) + jnp.log1p(jnp.exp(-jnp.abs(dt_raw)))
    a = -jnp.exp(alog_ref[...])
    da = dt * a

    row = lax.broadcasted_iota(jnp.int32, (q, q), 0)
    col = lax.broadcasted_iota(jnp.int32, (q, q), 1)
    causal = (col >= row) if rev else (col <= row)
    tri = causal.astype(F32)
    cs_col = jnp.dot(tri, da, preferred_element_type=F32, precision=lax.Precision.HIGHEST)
    eye_h = (lax.broadcasted_iota(jnp.int32, (heads, heads), 0)
             == lax.broadcasted_iota(jnp.int32, (heads, heads), 1)).astype(F32)
    nt = (((1,), (1,)), ((), ()))
    cs_row = lax.dot_general(eye_h, cs_col, nt, preferred_element_type=F32,
                             precision=lax.Precision.HIGHEST)
    dt_row = lax.dot_general(eye_h, dt, nt, preferred_element_type=F32,
                             precision=lax.Precision.HIGHEST)
    last = 0 if rev else q - 1
    cs_last = cs_col[last:last + 1, :]

    e_lane = lax.broadcasted_iota(jnp.int32, (heads, d), 1)
    e_lo = lax.broadcasted_iota(jnp.int32, (heads, d), 0) * p_dim
    expand = jnp.where((e_lane >= e_lo) & (e_lane < e_lo + p_dim), 1.0, 0.0).astype(BF16)
    ecs_exp = _split_dot(jnp.exp(cs_col), expand)
    dec_exp = _split_dot(jnp.exp(cs_last - cs_col) * dt, expand)

    lane = lax.broadcasted_iota(jnp.int32, (1, gw), 1)

    for g in range(groups):
        xg_b = conv_silu(g * gw, gw)
        bg = conv_silu(d + g * n, n)
        cg = conv_silu(d + groups * n + g * n, n)
        gsl = slice(g * gw, (g + 1) * gw)
        xg = xg_b.astype(F32)

        scores = lax.dot_general(cg, bg, nt, preferred_element_type=F32)
        m_parts = []
        r_parts = []
        for hh in range(hpg):
            h = g * hpg + hh
            diff = cs_col[:, h:h + 1] - cs_row[h:h + 1, :]
            decay = jnp.exp(jnp.where(causal, diff, NEG_BIG))
            m_parts.append((scores * decay * dt_row[h:h + 1, :]).astype(BF16))
            in_head = (lane >= hh * p_dim) & (lane < (hh + 1) * p_dim)
            r_parts.append(jnp.where(in_head, xg_b, jnp.zeros_like(xg_b)))
        m_cat = jnp.concatenate(m_parts, axis=1)
        r_cat = jnp.concatenate(r_parts, axis=0)
        y = jnp.dot(m_cat, r_cat, preferred_element_type=F32)

        st = state_ref[g]
        y = y + jnp.dot(cg, st.astype(BF16), preferred_element_type=F32) * ecs_exp[:, gsl]
        contrib = lax.dot_general(bg, (xg * dec_exp[:, gsl]).astype(BF16),
                                  (((0,), (0,)), ((), ())), preferred_element_type=F32)
        state_ref[g] = st * ecs_exp[last:last + 1, gsl] + contrib

        if final:
            y = y + yother_ref[0][:, gsl] + dskip_ref[:, gsl] * xg
            y = y * _silu(z_ref[0][:, gsl].astype(F32))
            ms = jnp.mean(y * y, axis=-1, keepdims=True)
            y = y * lax.rsqrt(ms + EPS) * gw_ref[:, gsl]
        o_ref[0, :, gsl] = y.astype(o_ref.dtype)


def _ssd_specs(s, d, rev):
    q = SSM_CHUNK
    n_chunks = s // q
    hb = q // SSM_HALO
    n_hblk = s // SSM_HALO

    def cidx(c):
        return (n_chunks - 1 - c) if rev else c

    def main(colblk, width=d):
        return pl.BlockSpec((1, q, width), lambda bi, c: (bi, cidx(c), colblk))

    def prev(colblk):
        return pl.BlockSpec((1, SSM_HALO, d),
                            lambda bi, c: (bi, jnp.maximum(cidx(c) * hb - 1, 0), colblk))

    def nxt(colblk):
        return pl.BlockSpec((1, SSM_HALO, d),
                            lambda bi, c: (bi, jnp.minimum(cidx(c) * hb + hb, n_hblk - 1), colblk))

    def whole(arr):
        return pl.BlockSpec(arr.shape, lambda bi, c: (0,) * arr.ndim)

    return n_chunks, main, prev, nxt, whole


def _ssd_first(u, dt_raw, conv_w, conv_b, dt_bias, a_log, *, heads, groups, d):
    b, s, _ = u.shape
    n_chunks, main, prev, nxt, whole = _ssd_specs(s, d, rev=True)
    params = (conv_w, conv_b.reshape(1, -1), dt_bias.reshape(1, -1), a_log.reshape(1, -1))
    return pl.pallas_call(
        functools.partial(_ssd_kernel, rev=True, final=False, n_chunks=n_chunks, d=d,
                          heads=heads, groups=groups, conv_k=conv_w.shape[0]),
        grid=(b, n_chunks),
        in_specs=[main(1), main(2), prev(1), prev(2), nxt(1), nxt(2), main(0, dt_raw.shape[-1])]
                 + [whole(a) for a in params],
        out_specs=[main(0), main(0, 2 * d)],
        out_shape=[jax.ShapeDtypeStruct((b, s, d), F32),
                   jax.ShapeDtypeStruct((b, s, 2 * d), BF16)],
        scratch_shapes=[pltpu.VMEM((SSM_CHUNK + 2 * SSM_HALO, 2 * d), BF16),
                        pltpu.VMEM((groups, SSM_STATE, d // groups), F32)],
        compiler_params=_cparams(("parallel", "arbitrary")),
        name="ssd_first",
    )(u, u, u, u, u, u, dt_raw, *params)


def _ssd_final(u, xbc, dt_raw, y_other, dt_bias, a_log, d_skip_exp, gnorm_w, *, heads, groups, d):
    b, s, _ = u.shape
    n_chunks, main, _, _, whole = _ssd_specs(s, d, rev=False)
    params = (dt_bias.reshape(1, -1), a_log.reshape(1, -1), d_skip_exp.reshape(1, -1),
              gnorm_w.reshape(1, -1))
    return pl.pallas_call(
        functools.partial(_ssd_kernel, rev=False, final=True, n_chunks=n_chunks, d=d,
                          heads=heads, groups=groups, conv_k=0),
        grid=(b, n_chunks),
        in_specs=[main(0), main(0, 2 * d), main(0, dt_raw.shape[-1]), main(0)]
                 + [whole(a) for a in params],
        out_specs=main(0),
        out_shape=jax.ShapeDtypeStruct((b, s, d), BF16),
        scratch_shapes=[pltpu.VMEM((groups, SSM_STATE, d // groups), F32)],
        compiler_params=_cparams(("parallel", "arbitrary")),
        name="ssd_final",
    )(u, xbc, dt_raw, y_other, *params)


def _na_kernel(q_ref, k_ref, v_ref, g_ref, tbl_ref, o_ref, bias_ref, *, rows, kr, ro_base):
    w = GRID_W
    for dlt in range(kr):
        for i in range(0, kr, 2):
            pair = [tbl_ref[0, ro_base + dlt + i + j] for j in range(2)]
            bias_ref[dlt, :, i * w:(i + 2) * w] = jnp.concatenate(pair, axis=-1)

    def body(it, carry):
        rr = [it * NA_ROW_UNROLL + u for u in range(NA_ROW_UNROLL)]
        rs = [jnp.clip(r - kr // 2, 0, rows - kr) for r in rr]
        q0 = [pl.multiple_of(r * w, w) for r in rr]
        k0 = [pl.multiple_of(x * w, w) for x in rs]
        s = [lax.dot_general(q_ref[0, pl.ds(q0[u], w), :], k_ref[0, pl.ds(k0[u], kr * w), :],
                             (((1,), (1,)), ((), ())), preferred_element_type=F32)
             for u in range(NA_ROW_UNROLL)]
        s = [s[u] + bias_ref[rs[u] - rr[u] + kr - 1] for u in range(NA_ROW_UNROLL)]
        e = [jnp.exp(x - jnp.max(x, axis=-1, keepdims=True)) for x in s]
        l = [jnp.sum(x, axis=-1, keepdims=True) for x in e]
        o = [jnp.dot(e[u].astype(BF16), v_ref[0, pl.ds(k0[u], kr * w), :],
                     preferred_element_type=F32) for u in range(NA_ROW_UNROLL)]
        for u in range(NA_ROW_UNROLL):
            gate = g_ref[0, pl.ds(q0[u], w), :].astype(F32)
            o_ref[0, pl.ds(q0[u], w), :] = (o[u] / l[u] * gate).astype(o_ref.dtype)
        return carry

    lax.fori_loop(0, rows // NA_ROW_UNROLL, body, 0)


def _na_bias_table(rpb):
    win_cols = (rpb.shape[2] + 1) // 2
    cols = jnp.arange(GRID_W)
    col_start = jnp.clip(cols - win_cols // 2, 0, GRID_W - win_cols)
    col_valid = (cols[None, :] >= col_start[:, None]) & (cols[None, :] < col_start[:, None] + win_cols)
    col_off = jnp.clip(cols[None, :] - cols[:, None], -(win_cols - 1), win_cols - 1) + win_cols - 1
    return jnp.where(col_valid[None, None], rpb.astype(F32)[:, :, col_off], NEG_BIG)


def _na_attention(u, rpb, *, b):
    n_sec, t, hd = u.shape
    heads = n_sec // 4
    s = t // b
    rows = s // GRID_W
    n_ro = rpb.shape[1]
    win_rows = (n_ro + 1) // 2
    kr = min(win_rows, rows)
    assert kr % 2 == 0 and 2 * GRID_W == LANES and rows % NA_ROW_UNROLL == 0
    tbl = _na_bias_table(rpb)

    def sec(i):
        return pl.BlockSpec((1, s, hd), lambda bi, h: (i * heads + h, bi, 0))

    return pl.pallas_call(
        functools.partial(_na_kernel, rows=rows, kr=kr, ro_base=win_rows - kr),
        grid=(b, heads),
        in_specs=[sec(0), sec(1), sec(2), sec(3),
                  pl.BlockSpec((1, n_ro, GRID_W, GRID_W), lambda bi, h: (h, 0, 0, 0))],
        out_specs=pl.BlockSpec((1, s, hd), lambda bi, h: (h, bi, 0)),
        out_shape=jax.ShapeDtypeStruct((heads, t, hd), BF16),
        scratch_shapes=[pltpu.VMEM((kr, GRID_W, kr * GRID_W), F32)],
        compiler_params=_cparams(("parallel", "arbitrary")),
        name="neighbourhood_attention",
    )(u, u, u, u, tbl)


def _conformer_kernel(v_ref, vp_ref, vn_ref, g_ref, w_ref, b_ref, lnw_ref, lnb_ref,
                      o_ref, xs_ref, conv_ref, *, n_blocks, width, rows):
    ci = pl.program_id(1)
    hl = CONF_HALO
    n_strips = v_ref.shape[0]
    seg = rows // SUBLANES
    p_in = seg + 2 * hl + SUBLANES
    p_out = seg + SUBLANES
    half = width // 2
    has_prev = (ci > 0).astype(F32)
    has_next = (ci < n_blocks - 1).astype(F32)

    for j in range(SUBLANES):
        base = p_in * j
        lo = seg * j - hl
        hi = seg * j + seg + hl
        if lo < 0:
            xs_ref[:, base:base + hl, :] = vp_ref[...].astype(F32) * has_prev
            xs_ref[:, base + hl:base + hl + hi, :] = v_ref[:, 0:hi, :].astype(F32)
        elif hi > rows:
            xs_ref[:, base:base + rows - lo, :] = v_ref[:, lo:rows, :].astype(F32)
            xs_ref[:, base + rows - lo:base + seg + 2 * hl, :] = vn_ref[...].astype(F32) * has_next
        else:
            xs_ref[:, base:base + seg + 2 * hl, :] = v_ref[:, lo:hi, :].astype(F32)

    off = hl - half

    def strip_body(s, carry):
        wv = [jnp.broadcast_to(w_ref[k, s], (SUBLANES, LANES)) for k in range(width)]
        bv = jnp.broadcast_to(b_ref[s], (SUBLANES, LANES))
        for i0 in range(0, seg, CONF_ACC):
            acc = [bv] * CONF_ACC
            for m in range(i0 + off, i0 + CONF_ACC - 1 + off + width):
                xv = xs_ref[s, pl.ds(m, SUBLANES, stride=p_in), :]
                for i in range(i0, i0 + CONF_ACC):
                    k = m - i - off
                    if 0 <= k < width:
                        acc[i - i0] = acc[i - i0] + wv[k] * xv
            for i in range(i0, i0 + CONF_ACC):
                conv_ref[s, pl.ds(i, SUBLANES, stride=p_out), :] = acc[i - i0]
        return carry

    lax.fori_loop(0, n_strips, strip_body, 0)

    inv_dc = 1.0 / (n_strips * LANES)

    def norm_body(j, carry):
        for rb in range(seg // BF16_ROWS):
            r_in = pl.multiple_of(j * p_out + rb * BF16_ROWS, SUBLANES)
            t0 = pl.multiple_of(j * seg + rb * BF16_ROWS, BF16_ROWS)
            x = conv_ref[:, pl.ds(r_in, BF16_ROWS), :]
            mu = jnp.sum(jnp.sum(x, axis=0), axis=-1, keepdims=True) * inv_dc
            xc = x - mu[None]
            var = jnp.sum(jnp.sum(xc * xc, axis=0), axis=-1, keepdims=True) * inv_dc
            y = _silu(xc * lax.rsqrt(var + EPS)[None] * lnw_ref[...] + lnb_ref[...])
            y = jnp.concatenate([y[s] for s in range(n_strips)], axis=-1).astype(BF16)
            o_ref[0, pl.ds(t0, BF16_ROWS), :] = y * g_ref[0, pl.ds(t0, BF16_ROWS), :]
        return carry

    lax.fori_loop(0, SUBLANES, norm_body, 0)


def _conformer_mix(v, g, dw_w, dw_b, ln_w, ln_b, *, b, rows=256):
    n_strips, t, _ = v.shape
    dc = n_strips * LANES
    s = t // b
    rows = min(rows, s)
    n_blocks = s // rows
    hb = rows // CONF_HALO
    n_hblk = s // CONF_HALO
    width = dw_w.shape[0]
    seg = rows // SUBLANES
    assert width // 2 <= CONF_HALO <= seg and seg % BF16_ROWS == 0 and seg % CONF_ACC == 0

    def whole(arr):
        return pl.BlockSpec(arr.shape, lambda bi, c: (0,) * arr.ndim)

    params = (dw_w.reshape(width, n_strips, 1, LANES), dw_b.reshape(n_strips, 1, LANES),
              ln_w.reshape(n_strips, 1, LANES), ln_b.reshape(n_strips, 1, LANES))
    return pl.pallas_call(
        functools.partial(_conformer_kernel, n_blocks=n_blocks, width=width, rows=rows),
        grid=(b, n_blocks),
        in_specs=[pl.BlockSpec((n_strips, rows, LANES), lambda bi, c: (0, bi * n_blocks + c, 0)),
                  pl.BlockSpec((n_strips, CONF_HALO, LANES),
                               lambda bi, c: (0, bi * n_hblk + jnp.maximum(c * hb - 1, 0), 0)),
                  pl.BlockSpec((n_strips, CONF_HALO, LANES),
                               lambda bi, c: (0, bi * n_hblk + jnp.minimum(c * hb + hb, n_hblk - 1), 0)),
                  pl.BlockSpec((1, rows, dc), lambda bi, c: (bi, c, 0))]
                 + [whole(a) for a in params],
        out_specs=pl.BlockSpec((1, rows, dc), lambda bi, c: (bi, c, 0)),
        out_shape=jax.ShapeDtypeStruct((b, s, dc), BF16),
        scratch_shapes=[pltpu.VMEM((n_strips, SUBLANES * (seg + 2 * CONF_HALO + SUBLANES), LANES), F32),
                        pltpu.VMEM((n_strips, SUBLANES * (seg + SUBLANES), LANES), F32)],
        compiler_params=_cparams(("parallel", "arbitrary")),
        name="conformer_conv_ln",
    )(v, v, v, g.reshape(b, s, dc), *params)


def _even_layer(h, b, s, norm_w, w_in, conv_w, conv_b, dt_bias_f, dt_bias_b, a_log_f, a_log_b,
                d_skip, gnorm_w, rpb, w_out):
    t, d = h.shape
    heads = d_skip.shape[0]
    conv_ch = conv_w.shape[1]
    groups = (conv_ch - d) // (2 * SSM_STATE)
    assert conv_ch == 2 * d, "SSD conv channels are fetched as two d-wide column blocks"
    na_heads = rpb.shape[0]
    n_dt = 2 * heads
    c_dt = d + conv_ch
    c_q = c_dt + n_dt

    w_ssd = w_in[:, :c_dt].astype(BF16)
    dt_pad = (-n_dt) % LANES
    w_dt = jnp.pad(w_in[:, c_dt:c_q], ((0, 0), (0, dt_pad))).astype(BF16)
    q_scale = (d // na_heads) ** -0.5
    w_na = jnp.concatenate([w_in[:, c_q:c_q + d] * q_scale, w_in[:, c_q + d:]], axis=1).astype(BF16)

    u = _norm_matmul(h, norm_w, w_ssd, BF16).reshape(b, s, -1)
    dt_raw = _norm_matmul(h, norm_w, w_dt, F32, tn=LANES).reshape(b, s, -1)
    assert d // na_heads == LANES, "attention heads are exchanged as 128-lane column strips"
    u_na = _norm_matmul(h, norm_w, w_na, BF16, strips=True, silu_from_col=3 * d)

    d_skip_exp = jnp.repeat(d_skip, d // heads)
    common = dict(heads=heads, groups=groups, d=d)
    y_b, xbc = _ssd_first(u, dt_raw, conv_w, conv_b, dt_bias_b, a_log_b, **common)
    y_ssd = _ssd_final(u, xbc, dt_raw, y_b, dt_bias_f, a_log_f, d_skip_exp, gnorm_w, **common)
    y_na = _na_attention(u_na, rpb, b=b)

    return _proj_residual(h, [y_ssd.reshape(t, d), y_na], w_out.astype(BF16))


def _odd_layer(h, b, s, norm_w, w_in, dw_w, dw_b, ln_w, ln_b, w_out):
    v, g = _conformer_in(h, norm_w, w_in.astype(BF16))
    y = _conformer_mix(v, g, dw_w, dw_b, ln_w, ln_b, b=b)
    return _proj_residual(h, [y.reshape(h.shape[0], -1)], w_out.astype(BF16))


def kernel(x, p, ev_norm_w, ev_w_in, ev_conv_w, ev_conv_b, ev_dt_bias_f, ev_dt_bias_b, ev_a_log_f, ev_a_log_b, ev_d_skip, ev_gnorm_w, ev_rpb, ev_w_out, od_norm_w, od_w_in, od_dw_w, od_dw_b, od_ln_w, od_ln_b, od_w_out, ple_norm_w, ple_w_gate, ple_w_proj, final_norm_w):
    b, s, d = x.shape
    depth = p.shape[0]
    h = x.reshape(b * s, d)
    for i in range(depth):
        e = i // 2
        if i % 2 == 0:
            h = _even_layer(h, b, s, ev_norm_w[e], ev_w_in[e], ev_conv_w[e], ev_conv_b[e],
                            ev_dt_bias_f[e], ev_dt_bias_b[e], ev_a_log_f[e], ev_a_log_b[e],
                            ev_d_skip[e], ev_gnorm_w[e], ev_rpb[e], ev_w_out[e])
        else:
            h = _odd_layer(h, b, s, od_norm_w[e], od_w_in[e], od_dw_w[e], od_dw_b[e],
                           od_ln_w[e], od_ln_b[e], od_w_out[e])
        h = _ple(h, ple_norm_w[i], ple_w_gate[i].astype(BF16), p[i].reshape(b * s, -1),
                 ple_w_proj[i].astype(BF16))
    return _final_norm(h, final_norm_w).reshape(b, s, d)
```

```python
import functools

import jax
import jax.numpy as jnp
from jax import lax
from jax.experimental import pallas as pl
from jax.experimental.pallas import tpu as pltpu

F32 = jnp.float32
BF16 = jnp.bfloat16

EPS = 1e-6
LANES = 128
SUBLANES = 8
GRID_W = 64
SSM_STATE = 128
SSM_CHUNK = 128
SSM_HALO = 16
SSM_CONV_COLS = 256
CONF_HALO = 16
CONF_ACC = 16
NEG_BIG = -1e30
BF16_ROWS = 16
NORM_CHUNKS = 4
NA_ROW_UNROLL = 8

V7X_VMEM_LIMIT = 56 * 1024 * 1024


def _cparams(sem):
    return pltpu.CompilerParams(dimension_semantics=sem, vmem_limit_bytes=V7X_VMEM_LIMIT)


def _sigmoid(x):
    return 1.0 / (1.0 + jnp.exp(-x))


def _silu(x):
    return x * _sigmoid(x)


def _rms_rows(x, w):
    ms = jnp.mean(x * x, axis=-1, keepdims=True)
    return x * lax.rsqrt(ms + EPS) * w


def _store_strips(o_ref, rows, val):
    for c in range(o_ref.shape[0]):
        o_ref[c, rows, :] = val[:, c * LANES:(c + 1) * LANES].astype(o_ref.dtype)


def _load_strips(x_ref):
    return jnp.concatenate([x_ref[c] for c in range(x_ref.shape[0])], axis=-1)


def _for_normed_rows(h_ref, nw_ref, xn_ref, emit):
    tm = h_ref.shape[0]
    rc = tm // NORM_CHUNKS

    @pl.when(pl.program_id(1) == 0)
    def _():
        for c in range(NORM_CHUNKS):
            rows = slice(c * rc, (c + 1) * rc)
            xn = _rms_rows(h_ref[rows, :], nw_ref[...]).astype(BF16)
            xn_ref[rows, :] = xn
            emit(rows, xn)

    @pl.when(pl.program_id(1) > 0)
    def _():
        emit(slice(0, tm), xn_ref[...])


def _norm_matmul_kernel(h_ref, nw_ref, w_ref, o_ref, xn_ref, *, strips, silu_from):
    def emit(rows, xn):
        acc = jnp.dot(xn, w_ref[...], preferred_element_type=F32)
        if not strips:
            o_ref[rows, :] = acc.astype(o_ref.dtype)
        elif silu_from is None:
            _store_strips(o_ref, rows, acc)
        else:
            @pl.when(pl.program_id(1) < silu_from)
            def _():
                _store_strips(o_ref, rows, acc)

            @pl.when(pl.program_id(1) >= silu_from)
            def _():
                _store_strips(o_ref, rows, _silu(acc))

    _for_normed_rows(h_ref, nw_ref, xn_ref, emit)


def _norm_matmul(h, norm_w, w, out_dtype, *, strips=False, silu_from_col=None, tm=1024, tn=512):
    t, d = h.shape
    n = w.shape[1]
    tm = min(tm, t)
    tn = min(tn, n)
    assert t % tm == 0 and n % tn == 0 and tn % LANES == 0
    silu_from = None
    if silu_from_col is not None:
        assert strips and silu_from_col % tn == 0
        silu_from = silu_from_col // tn
    if strips:
        out_spec = pl.BlockSpec((tn // LANES, tm, LANES), lambda i, j: (j, i, 0))
        out_shape = jax.ShapeDtypeStruct((n // LANES, t, LANES), out_dtype)
    else:
        out_spec = pl.BlockSpec((tm, tn), lambda i, j: (i, j))
        out_shape = jax.ShapeDtypeStruct((t, n), out_dtype)
    return pl.pallas_call(
        functools.partial(_norm_matmul_kernel, strips=strips, silu_from=silu_from),
        grid=(t // tm, n // tn),
        in_specs=[pl.BlockSpec((tm, d), lambda i, j: (i, 0)),
                  pl.BlockSpec((1, d), lambda i, j: (0, 0)),
                  pl.BlockSpec((d, tn), lambda i, j: (0, j))],
        out_specs=out_spec,
        out_shape=out_shape,
        scratch_shapes=[pltpu.VMEM((tm, d), BF16)],
        compiler_params=_cparams(("parallel", "arbitrary")),
        name="norm_matmul_strips" if strips else "norm_matmul",
    )(h, norm_w.reshape(1, d), w)


def _conformer_in_kernel(h_ref, nw_ref, wa_ref, wb_ref, wg_ref, v_ref, g_ref, xn_ref):
    def emit(rows, xn):
        a = jnp.dot(xn, wa_ref[...], preferred_element_type=F32)
        a = a * _sigmoid(jnp.dot(xn, wb_ref[...], preferred_element_type=F32))
        _store_strips(v_ref, rows, a)
        g = _silu(jnp.dot(xn, wg_ref[...], preferred_element_type=F32))
        g_ref[rows, :] = g.astype(g_ref.dtype)

    _for_normed_rows(h_ref, nw_ref, xn_ref, emit)


def _conformer_in(h, norm_w, w, *, tm=1024, tn=512):
    t, d = h.shape
    n = w.shape[1] // 3
    tm = min(tm, t)
    tn = min(tn, n)
    assert t % tm == 0 and n % tn == 0 and tn % LANES == 0
    nb = n // tn

    def w_spec(section):
        return pl.BlockSpec((d, tn), lambda i, j: (0, section * nb + j))

    return pl.pallas_call(
        _conformer_in_kernel,
        grid=(t // tm, n // tn),
        in_specs=[pl.BlockSpec((tm, d), lambda i, j: (i, 0)),
                  pl.BlockSpec((1, d), lambda i, j: (0, 0)),
                  w_spec(0), w_spec(1), w_spec(2)],
        out_specs=[pl.BlockSpec((tn // LANES, tm, LANES), lambda i, j: (j, i, 0)),
                   pl.BlockSpec((tm, tn), lambda i, j: (i, j))],
        out_shape=[jax.ShapeDtypeStruct((n // LANES, t, LANES), BF16),
                   jax.ShapeDtypeStruct((t, n), BF16)],
        scratch_shapes=[pltpu.VMEM((tm, d), BF16)],
        compiler_params=_cparams(("parallel", "arbitrary")),
        name="conformer_in_proj",
    )(h, norm_w.reshape(1, d), w, w, w)


def _proj_residual_kernel(*refs, n_x):
    h_ref = refs[0]
    x_refs = refs[1:1 + n_x]
    w_refs = refs[1 + n_x:1 + 2 * n_x]
    o_ref = refs[1 + 2 * n_x]
    acc = h_ref[...]
    for x_ref, w_ref in zip(x_refs, w_refs):
        x = _load_strips(x_ref) if len(x_ref.shape) == 3 else x_ref[...]
        acc = acc + jnp.dot(x, w_ref[...], preferred_element_type=F32)
    o_ref[...] = acc


def _proj_residual(h, xs, w, *, tm=1024, tn=512):
    t, n = h.shape
    tm = min(tm, t)
    tn = min(tn, n)
    assert t % tm == 0 and n % tn == 0
    k = w.shape[0] // len(xs)
    in_specs = [pl.BlockSpec((tm, tn), lambda i, j: (i, j))]
    for x in xs:
        if x.ndim == 3:
            assert x.shape[0] * LANES == k
            in_specs.append(pl.BlockSpec((x.shape[0], tm, LANES), lambda i, j: (0, i, 0)))
        else:
            assert x.shape[1] == k
            in_specs.append(pl.BlockSpec((tm, k), lambda i, j: (i, 0)))
    in_specs += [pl.BlockSpec((k, tn), lambda i, j, r=r: (r, j)) for r in range(len(xs))]
    return pl.pallas_call(
        functools.partial(_proj_residual_kernel, n_x=len(xs)),
        grid=(t // tm, n // tn),
        in_specs=in_specs,
        out_specs=pl.BlockSpec((tm, tn), lambda i, j: (i, j)),
        out_shape=jax.ShapeDtypeStruct((t, n), F32),
        compiler_params=_cparams(("parallel", "arbitrary")),
        name="proj_residual",
    )(h, *xs, *([w] * len(xs)))


def _ple_kernel(hrow_ref, htile_ref, nw_ref, wg_ref, p_ref, wp_ref, o_ref, xn_ref):
    def emit(rows, xn):
        gate = _sigmoid(jnp.dot(xn, wg_ref[...], preferred_element_type=F32))
        proj = jnp.dot(p_ref[rows, :].astype(BF16), wp_ref[...], preferred_element_type=F32)
        o_ref[rows, :] = htile_ref[rows, :] + gate * proj

    _for_normed_rows(hrow_ref, nw_ref, xn_ref, emit)


def _ple(h, norm_w, w_gate, p, w_proj, *, tm=1024, tn=512):
    t, d = h.shape
    e = p.shape[1]
    tm = min(tm, t)
    tn = min(tn, d)
    assert t % tm == 0 and d % tn == 0
    return pl.pallas_call(
        _ple_kernel,
        grid=(t // tm, d // tn),
        in_specs=[pl.BlockSpec((tm, d), lambda i, j: (i, 0)),
                  pl.BlockSpec((tm, tn), lambda i, j: (i, j)),
                  pl.BlockSpec((1, d), lambda i, j: (0, 0)),
                  pl.BlockSpec((d, tn), lambda i, j: (0, j)),
                  pl.BlockSpec((tm, e), lambda i, j: (i, 0)),
                  pl.BlockSpec((e, tn), lambda i, j: (0, j))],
        out_specs=pl.BlockSpec((tm, tn), lambda i, j: (i, j)),
        out_shape=jax.ShapeDtypeStruct((t, d), F32),
        scratch_shapes=[pltpu.VMEM((tm, d), BF16)],
        compiler_params=_cparams(("parallel", "arbitrary")),
        name="per_layer_embedding",
    )(h, h, norm_w.reshape(1, d), w_gate, p, w_proj)


def _final_norm_kernel(h_ref, w_ref, o_ref):
    o_ref[...] = _rms_rows(h_ref[...], w_ref[...])


def _final_norm(h, w, *, tm=512):
    t, d = h.shape
    tm = min(tm, t)
    return pl.pallas_call(
        _final_norm_kernel,
        grid=(t // tm,),
        in_specs=[pl.BlockSpec((tm, d), lambda i: (i, 0)),
                  pl.BlockSpec((1, d), lambda i: (0, 0))],
        out_specs=pl.BlockSpec((tm, d), lambda i: (i, 0)),
        out_shape=jax.ShapeDtypeStruct((t, d), F32),
        compiler_params=_cparams(("parallel",)),
        name="final_rmsnorm",
    )(h, w.reshape(1, d))


def _split_dot(v, e):
    hi = v.astype(BF16)
    lo = (v - hi.astype(F32)).astype(BF16)
    return (jnp.dot(hi, e, preferred_element_type=F32)
            + jnp.dot(lo, e, preferred_element_type=F32))


def _ssd_kernel(*refs, rev, final, n_chunks, d, heads, groups, conv_k):
    if final:
        (z_ref, xbc_ref, dt_ref, yother_ref,
         dtb_ref, alog_ref, dskip_ref, gw_ref, o_ref, state_ref) = refs
    else:
        (x_ref, bc_ref, xp_ref, bcp_ref, xn_ref, bcn_ref, dt_ref,
         cw_ref, cb_ref, dtb_ref, alog_ref, o_ref, xbc_out_ref, xpad_ref, state_ref) = refs

    q = SSM_CHUNK
    n = SSM_STATE
    hpg = heads // groups
    gw = d // groups
    p_dim = d // heads
    c = pl.program_id(1)
    ci = (n_chunks - 1 - c) if rev else c

    @pl.when(c == 0)
    def _():
        state_ref[...] = jnp.zeros_like(state_ref)

    if final:
        def conv_silu(lo, width):
            return xbc_ref[0, :, lo:lo + width]
    else:
        hl = SSM_HALO
        no_halo = jnp.zeros((hl, d), BF16)
        xpad_ref[0:hl, 0:d] = jnp.where(ci > 0, xp_ref[0], no_halo)
        xpad_ref[0:hl, d:2 * d] = jnp.where(ci > 0, bcp_ref[0], no_halo)
        xpad_ref[hl:hl + q, 0:d] = x_ref[0]
        xpad_ref[hl:hl + q, d:2 * d] = bc_ref[0]
        xpad_ref[hl + q:, 0:d] = jnp.where(ci < n_chunks - 1, xn_ref[0], no_halo)
        xpad_ref[hl + q:, d:2 * d] = jnp.where(ci < n_chunks - 1, bcn_ref[0], no_halo)

        half = conv_k // 2
        taps = [k for k in range(conv_k) if k != half]
        sel_row = lax.broadcasted_iota(jnp.int32, (q, q + 2 * hl), 0)
        sel_col = lax.broadcasted_iota(jnp.int32, (q, q + 2 * hl), 1)
        sel = jnp.concatenate([jnp.where(sel_col == sel_row + (hl + k - half), 1.0, 0.0).astype(BF16)
                               for k in taps], axis=0)
        for lo in range(0, 2 * d, SSM_CONV_COLS):
            csl = slice(lo, lo + SSM_CONV_COLS)
            xc = xpad_ref[:, csl]
            shifted = jnp.dot(sel, xc, preferred_element_type=F32)
            acc = cb_ref[:, csl] + cw_ref[half:half + 1, csl] * xc[hl:hl + q].astype(F32)
            for kk, k in enumerate(taps):
                acc = acc + cw_ref[k:k + 1, csl] * shifted[kk * q:(kk + 1) * q]
            xbc_out_ref[0, :, csl] = _silu(acc).astype(BF16)

        def conv_silu(lo, width):
            return xbc_out_ref[0, :, lo:lo + width]

    dsel = heads if rev else 0
    dt_raw = dt_ref[0][:, dsel:dsel + heads] + dtb_ref[...]
    dt = jnp.maximum(dt_raw, 0.0) + jnp.log1p(jnp.exp(-jnp.abs(dt_raw)))
    a = -jnp.exp(alog_ref[...])
    da = dt * a

    row = lax.broadcasted_iota(jnp.int32, (q, q), 0)
    col = lax.broadcasted_iota(jnp.int32, (q, q), 1)
    causal = (col >= row) if rev else (col <= row)
    tri = causal.astype(F32)
    cs_col = jnp.dot(tri, da, preferred_element_type=F32, precision=lax.Precision.HIGHEST)
    eye_h = (lax.broadcasted_iota(jnp.int32, (heads, heads), 0)
             == lax.broadcasted_iota(jnp.int32, (heads, heads), 1)).astype(F32)
    nt = (((1,), (1,)), ((), ()))
    cs_row = lax.dot_general(eye_h, cs_col, nt, preferred_element_type=F32,
                             precision=lax.Precision.HIGHEST)
    dt_row = lax.dot_general(eye_h, dt, nt, preferred_element_type=F32,
                             precision=lax.Precision.HIGHEST)
    last = 0 if rev else q - 1
    cs_last = cs_col[last:last + 1, :]

    e_lane = lax.broadcasted_iota(jnp.int32, (heads, d), 1)
    e_lo = lax.broadcasted_iota(jnp.int32, (heads, d), 0) * p_dim
    expand = jnp.where((e_lane >= e_lo) & (e_lane < e_lo + p_dim), 1.0, 0.0).astype(BF16)
    ecs_exp = _split_dot(jnp.exp(cs_col), expand)
    dec_exp = _split_dot(jnp.exp(cs_last - cs_col) * dt, expand)

    lane = lax.broadcasted_iota(jnp.int32, (1, gw), 1)

    for g in range(groups):
        xg_b = conv_silu(g * gw, gw)
        bg = conv_silu(d + g * n, n)
        cg = conv_silu(d + groups * n + g * n, n)
        gsl = slice(g * gw, (g + 1) * gw)
        xg = xg_b.astype(F32)

        scores = lax.dot_general(cg, bg, nt, preferred_element_type=F32)
        m_parts = []
        r_parts = []
        for hh in range(hpg):
            h = g * hpg + hh
            diff = cs_col[:, h:h + 1] - cs_row[h:h + 1, :]
            decay = jnp.exp(jnp.where(causal, diff, NEG_BIG))
            m_parts.append((scores * decay * dt_row[h:h + 1, :]).astype(BF16))
            in_head = (lane >= hh * p_dim) & (lane < (hh + 1) * p_dim)
            r_parts.append(jnp.where(in_head, xg_b, jnp.zeros_like(xg_b)))
        m_cat = jnp.concatenate(m_parts, axis=1)
        r_cat = jnp.concatenate(r_parts, axis=0)
        y = jnp.dot(m_cat, r_cat, preferred_element_type=F32)

        st = state_ref[g]
        y = y + jnp.dot(cg, st.astype(BF16), preferred_element_type=F32) * ecs_exp[:, gsl]
        contrib = lax.dot_general(bg, (xg * dec_exp[:, gsl]).astype(BF16),
                                  (((0,), (0,)), ((), ())), preferred_element_type=F32)
        state_ref[g] = st * ecs_exp[last:last + 1, gsl] + contrib

        if final:
            y = y + yother_ref[0][:, gsl] + dskip_ref[:, gsl] * xg
            y = y * _silu(z_ref[0][:, gsl].astype(F32))
            ms = jnp.mean(y * y, axis=-1, keepdims=True)
            y = y * lax.rsqrt(ms + EPS) * gw_ref[:, gsl]
        o_ref[0, :, gsl] = y.astype(o_ref.dtype)


def _ssd_specs(s, d, rev):
    q = SSM_CHUNK
    n_chunks = s // q
    hb = q // SSM_HALO
    n_hblk = s // SSM_HALO

    def cidx(c):
        return (n_chunks - 1 - c) if rev else c

    def main(colblk, width=d):
        return pl.BlockSpec((1, q, width), lambda bi, c: (bi, cidx(c), colblk))

    def prev(colblk):
        return pl.BlockSpec((1, SSM_HALO, d),
                            lambda bi, c: (bi, jnp.maximum(cidx(c) * hb - 1, 0), colblk))

    def nxt(colblk):
        return pl.BlockSpec((1, SSM_HALO, d),
                            lambda bi, c: (bi, jnp.minimum(cidx(c) * hb + hb, n_hblk - 1), colblk))

    def whole(arr):
        return pl.BlockSpec(arr.shape, lambda bi, c: (0,) * arr.ndim)

    return n_chunks, main, prev, nxt, whole


def _ssd_first(u, dt_raw, conv_w, conv_b, dt_bias, a_log, *, heads, groups, d):
    b, s, _ = u.shape
    n_chunks, main, prev, nxt, whole = _ssd_specs(s, d, rev=True)
    params = (conv_w, conv_b.reshape(1, -1), dt_bias.reshape(1, -1), a_log.reshape(1, -1))
    return pl.pallas_call(
        functools.partial(_ssd_kernel, rev=True, final=False, n_chunks=n_chunks, d=d,
                          heads=heads, groups=groups, conv_k=conv_w.shape[0]),
        grid=(b, n_chunks),
        in_specs=[main(1), main(2), prev(1), prev(2), nxt(1), nxt(2), main(0, dt_raw.shape[-1])]
                 + [whole(a) for a in params],
        out_specs=[main(0), main(0, 2 * d)],
        out_shape=[jax.ShapeDtypeStruct((b, s, d), F32),
                   jax.ShapeDtypeStruct((b, s, 2 * d), BF16)],
        scratch_shapes=[pltpu.VMEM((SSM_CHUNK + 2 * SSM_HALO, 2 * d), BF16),
                        pltpu.VMEM((groups, SSM_STATE, d // groups), F32)],
        compiler_params=_cparams(("parallel", "arbitrary")),
        name="ssd_first",
    )(u, u, u, u, u, u, dt_raw, *params)


def _ssd_final(u, xbc, dt_raw, y_other, dt_bias, a_log, d_skip_exp, gnorm_w, *, heads, groups, d):
    b, s, _ = u.shape
    n_chunks, main, _, _, whole = _ssd_specs(s, d, rev=False)
    params = (dt_bias.reshape(1, -1), a_log.reshape(1, -1), d_skip_exp.reshape(1, -1),
              gnorm_w.reshape(1, -1))
    return pl.pallas_call(
        functools.partial(_ssd_kernel, rev=False, final=True, n_chunks=n_chunks, d=d,
                          heads=heads, groups=groups, conv_k=0),
        grid=(b, n_chunks),
        in_specs=[main(0), main(0, 2 * d), main(0, dt_raw.shape[-1]), main(0)]
                 + [whole(a) for a in params],
        out_specs=main(0),
        out_shape=jax.ShapeDtypeStruct((b, s, d), BF16),
        scratch_shapes=[pltpu.VMEM((groups, SSM_STATE, d // groups), F32)],
        compiler_params=_cparams(("parallel", "arbitrary")),
        name="ssd_final",
    )(u, xbc, dt_raw, y_other, *params)


def _na_kernel(q_ref, k_ref, v_ref, g_ref, tbl_ref, o_ref, bias_ref, *, rows, kr, ro_base):
    w = GRID_W
    for dlt in range(kr):
        for i in range(0, kr, 2):
            pair = [tbl_ref[0, ro_base + dlt + i + j] for j in range(2)]
            bias_ref[dlt, :, i * w:(i + 2) * w] = jnp.concatenate(pair, axis=-1)

    def body(it, carry):
        rr = [it * NA_ROW_UNROLL + u for u in range(NA_ROW_UNROLL)]
        rs = [jnp.clip(r - kr // 2, 0, rows - kr) for r in rr]
        q0 = [pl.multiple_of(r * w, w) for r in rr]
        k0 = [pl.multiple_of(x * w, w) for x in rs]
        s = [lax.dot_general(q_ref[0, pl.ds(q0[u], w), :], k_ref[0, pl.ds(k0[u], kr * w), :],
                             (((1,), (1,)), ((), ())), preferred_element_type=F32)
             for u in range(NA_ROW_UNROLL)]
        s = [s[u] + bias_ref[rs[u] - rr[u] + kr - 1] for u in range(NA_ROW_UNROLL)]
        e = [jnp.exp(x - jnp.max(x, axis=-1, keepdims=True)) for x in s]
        l = [jnp.sum(x, axis=-1, keepdims=True) for x in e]
        o = [jnp.dot(e[u].astype(BF16), v_ref[0, pl.ds(k0[u], kr * w), :],
                     preferred_element_type=F32) for u in range(NA_ROW_UNROLL)]
        for u in range(NA_ROW_UNROLL):
            gate = g_ref[0, pl.ds(q0[u], w), :].astype(F32)
            o_ref[0, pl.ds(q0[u], w), :] = (o[u] / l[u] * gate).astype(o_ref.dtype)
        return carry

    lax.fori_loop(0, rows // NA_ROW_UNROLL, body, 0)


def _na_bias_table(rpb):
    win_cols = (rpb.shape[2] + 1) // 2
    cols = jnp.arange(GRID_W)
    col_start = jnp.clip(cols - win_cols // 2, 0, GRID_W - win_cols)
    col_valid = (cols[None, :] >= col_start[:, None]) & (cols[None, :] < col_start[:, None] + win_cols)
    col_off = jnp.clip(cols[None, :] - cols[:, None], -(win_cols - 1), win_cols - 1) + win_cols - 1
    onehot = (col_off[None] == jnp.arange(rpb.shape[2])[:, None, None]).astype(F32)
    tbl = jnp.einsum("hrc,cqk->hrqk", rpb.astype(F32), onehot, precision=lax.Precision.HIGHEST)
    return jnp.where(col_valid[None, None], tbl, NEG_BIG)


def _na_attention(u, rpb, *, b):
    n_sec, t, hd = u.shape
    heads = n_sec // 4
    s = t // b
    rows = s // GRID_W
    n_ro = rpb.shape[1]
    win_rows = (n_ro + 1) // 2
    kr = min(win_rows, rows)
    assert kr % 2 == 0 and 2 * GRID_W == LANES and rows % NA_ROW_UNROLL == 0
    tbl = _na_bias_table(rpb)

    def sec(i):
        return pl.BlockSpec((1, s, hd), lambda bi, h: (i * heads + h, bi, 0))

    return pl.pallas_call(
        functools.partial(_na_kernel, rows=rows, kr=kr, ro_base=win_rows - kr),
        grid=(b, heads),
        in_specs=[sec(0), sec(1), sec(2), sec(3),
                  pl.BlockSpec((1, n_ro, GRID_W, GRID_W), lambda bi, h: (h, 0, 0, 0))],
        out_specs=pl.BlockSpec((1, s, hd), lambda bi, h: (h, bi, 0)),
        out_shape=jax.ShapeDtypeStruct((heads, t, hd), BF16),
        scratch_shapes=[pltpu.VMEM((kr, GRID_W, kr * GRID_W), F32)],
        compiler_params=_cparams(("parallel", "arbitrary")),
        name="neighbourhood_attention",
    )(u, u, u, u, tbl)


def _conformer_kernel(v_ref, vp_ref, vn_ref, g_ref, w_ref, b_ref, lnw_ref, lnb_ref,
                      o_ref, xs_ref, conv_ref, *, n_blocks, width, rows):
    ci = pl.program_id(1)
    hl = CONF_HALO
    n_strips = v_ref.shape[0]
    seg = rows // SUBLANES
    p_in = seg + 2 * hl + SUBLANES
    p_out = seg + SUBLANES
    half = width // 2
    has_prev = (ci > 0).astype(F32)
    has_next = (ci < n_blocks - 1).astype(F32)

    for j in range(SUBLANES):
        base = p_in * j
        lo = seg * j - hl
        hi = seg * j + seg + hl
        if lo < 0:
            xs_ref[:, base:base + hl, :] = vp_ref[...].astype(F32) * has_prev
            xs_ref[:, base + hl:base + hl + hi, :] = v_ref[:, 0:hi, :].astype(F32)
        elif hi > rows:
            xs_ref[:, base:base + rows - lo, :] = v_ref[:, lo:rows, :].astype(F32)
            xs_ref[:, base + rows - lo:base + seg + 2 * hl, :] = vn_ref[...].astype(F32) * has_next
        else:
            xs_ref[:, base:base + seg + 2 * hl, :] = v_ref[:, lo:hi, :].astype(F32)

    off = hl - half

    def strip_body(s, carry):
        wv = [jnp.broadcast_to(w_ref[k, s], (SUBLANES, LANES)) for k in range(width)]
        bv = jnp.broadcast_to(b_ref[s], (SUBLANES, LANES))
        for i0 in range(0, seg, CONF_ACC):
            acc = [bv] * CONF_ACC
            for m in range(i0 + off, i0 + CONF_ACC - 1 + off + width):
                xv = xs_ref[s, pl.ds(m, SUBLANES, stride=p_in), :]
                for i in range(i0, i0 + CONF_ACC):
                    k = m - i - off
                    if 0 <= k < width:
                        acc[i - i0] = acc[i - i0] + wv[k] * xv
            for i in range(i0, i0 + CONF_ACC):
                conv_ref[s, pl.ds(i, SUBLANES, stride=p_out), :] = acc[i - i0]
        return carry

    lax.fori_loop(0, n_strips, strip_body, 0)

    inv_dc = 1.0 / (n_strips * LANES)

    def norm_body(j, carry):
        for rb in range(seg // BF16_ROWS):
            r_in = pl.multiple_of(j * p_out + rb * BF16_ROWS, SUBLANES)
            t0 = pl.multiple_of(j * seg + rb * BF16_ROWS, BF16_ROWS)
            x = conv_ref[:, pl.ds(r_in, BF16_ROWS), :]
            mu = jnp.sum(jnp.sum(x, axis=0), axis=-1, keepdims=True) * inv_dc
            xc = x - mu[None]
            var = jnp.sum(jnp.sum(xc * xc, axis=0), axis=-1, keepdims=True) * inv_dc
            y = _silu(xc * lax.rsqrt(var + EPS)[None] * lnw_ref[...] + lnb_ref[...])
            y = jnp.concatenate([y[s] for s in range(n_strips)], axis=-1).astype(BF16)
            o_ref[0, pl.ds(t0, BF16_ROWS), :] = y * g_ref[0, pl.ds(t0, BF16_ROWS), :]
        return carry

    lax.fori_loop(0, SUBLANES, norm_body, 0)


def _conformer_mix(v, g, dw_w, dw_b, ln_w, ln_b, *, b, rows=256):
    n_strips, t, _ = v.shape
    dc = n_strips * LANES
    s = t // b
    rows = min(rows, s)
    n_blocks = s // rows
    hb = rows // CONF_HALO
    n_hblk = s // CONF_HALO
    width = dw_w.shape[0]
    seg = rows // SUBLANES
    assert width // 2 <= CONF_HALO <= seg and seg % BF16_ROWS == 0 and seg % CONF_ACC == 0

    def whole(arr):
        return pl.BlockSpec(arr.shape, lambda bi, c: (0,) * arr.ndim)

    params = (dw_w.reshape(width, n_strips, 1, LANES), dw_b.reshape(n_strips, 1, LANES),
              ln_w.reshape(n_strips, 1, LANES), ln_b.reshape(n_strips, 1, LANES))
    return pl.pallas_call(
        functools.partial(_conformer_kernel, n_blocks=n_blocks, width=width, rows=rows),
        grid=(b, n_blocks),
        in_specs=[pl.BlockSpec((n_strips, rows, LANES), lambda bi, c: (0, bi * n_blocks + c, 0)),
                  pl.BlockSpec((n_strips, CONF_HALO, LANES),
                               lambda bi, c: (0, bi * n_hblk + jnp.maximum(c * hb - 1, 0), 0)),
                  pl.BlockSpec((n_strips, CONF_HALO, LANES),
                               lambda bi, c: (0, bi * n_hblk + jnp.minimum(c * hb + hb, n_hblk - 1), 0)),
                  pl.BlockSpec((1, rows, dc), lambda bi, c: (bi, c, 0))]
                 + [whole(a) for a in params],
        out_specs=pl.BlockSpec((1, rows, dc), lambda bi, c: (bi, c, 0)),
        out_shape=jax.ShapeDtypeStruct((b, s, dc), BF16),
        scratch_shapes=[pltpu.VMEM((n_strips, SUBLANES * (seg + 2 * CONF_HALO + SUBLANES), LANES), F32),
                        pltpu.VMEM((n_strips, SUBLANES * (seg + SUBLANES), LANES), F32)],
        compiler_params=_cparams(("parallel", "arbitrary")),
        name="conformer_conv_ln",
    )(v, v, v, g.reshape(b, s, dc), *params)


def _even_layer(h, b, s, norm_w, w_in, conv_w, conv_b, dt_bias_f, dt_bias_b, a_log_f, a_log_b,
                d_skip, gnorm_w, rpb, w_out):
    t, d = h.shape
    heads = d_skip.shape[0]
    conv_ch = conv_w.shape[1]
    groups = (conv_ch - d) // (2 * SSM_STATE)
    assert conv_ch == 2 * d, "SSD conv channels are fetched as two d-wide column blocks"
    na_heads = rpb.shape[0]
    n_dt = 2 * heads
    c_dt = d + conv_ch
    c_q = c_dt + n_dt

    w_ssd = w_in[:, :c_dt].astype(BF16)
    dt_pad = (-n_dt) % LANES
    w_dt = jnp.pad(w_in[:, c_dt:c_q], ((0, 0), (0, dt_pad))).astype(BF16)
    q_scale = (d // na_heads) ** -0.5
    w_na = jnp.concatenate([w_in[:, c_q:c_q + d] * q_scale, w_in[:, c_q + d:]], axis=1).astype(BF16)

    u = _norm_matmul(h, norm_w, w_ssd, BF16).reshape(b, s, -1)
    dt_raw = _norm_matmul(h, norm_w, w_dt, F32, tn=LANES).reshape(b, s, -1)
    assert d // na_heads == LANES, "attention heads are exchanged as 128-lane column strips"
    u_na = _norm_matmul(h, norm_w, w_na, BF16, strips=True, silu_from_col=3 * d)

    d_skip_exp = jnp.repeat(d_skip, d // heads)
    common = dict(heads=heads, groups=groups, d=d)
    y_b, xbc = _ssd_first(u, dt_raw, conv_w, conv_b, dt_bias_b, a_log_b, **common)
    y_ssd = _ssd_final(u, xbc, dt_raw, y_b, dt_bias_f, a_log_f, d_skip_exp, gnorm_w, **common)
    y_na = _na_attention(u_na, rpb, b=b)

    return _proj_residual(h, [y_ssd.reshape(t, d), y_na], w_out.astype(BF16))


def _odd_layer(h, b, s, norm_w, w_in, dw_w, dw_b, ln_w, ln_b, w_out):
    v, g = _conformer_in(h, norm_w, w_in.astype(BF16))
    y = _conformer_mix(v, g, dw_w, dw_b, ln_w, ln_b, b=b)
    return _proj_residual(h, [y.reshape(h.shape[0], -1)], w_out.astype(BF16))


def kernel(x, p, ev_norm_w, ev_w_in, ev_conv_w, ev_conv_b, ev_dt_bias_f, ev_dt_bias_b, ev_a_log_f, ev_a_log_b, ev_d_skip, ev_gnorm_w, ev_rpb, ev_w_out, od_norm_w, od_w_in, od_dw_w, od_dw_b, od_ln_w, od_ln_b, od_w_out, ple_norm_w, ple_w_gate, ple_w_proj, final_norm_w):
    b, s, d = x.shape
    depth = p.shape[0]
    h = x.reshape(b * s, d)
    for i in range(depth):
        e = i // 2
        if i % 2 == 0:
            h = _even_layer(h, b, s, ev_norm_w[e], ev_w_in[e], ev_conv_w[e], ev_conv_b[e],
                            ev_dt_bias_f[e], ev_dt_bias_b[e], ev_a_log_f[e], ev_a_log_b[e],
                            ev_d_skip[e], ev_gnorm_w[e], ev_rpb[e], ev_w_out[e])
        else:
            h = _odd_layer(h, b, s, od_norm_w[e], od_w_in[e], od_dw_w[e], od_dw_b[e],
                           od_ln_w[e], od_ln_b[e], od_w_out[e])
        h = _ple(h, ple_norm_w[i], ple_w_gate[i].astype(BF16), p[i].reshape(b * s, -1),
                 ple_w_proj[i].astype(BF16))
    return _final_norm(h, final_norm_w).reshape(b, s, d)
```

```python
import functools
import math

import jax
import jax.numpy as jnp
from jax import lax
from jax.experimental import pallas as pl
from jax.experimental.pallas import tpu as pltpu

F32 = jnp.float32
BF16 = jnp.bfloat16

EPS = 1e-6
LANES = 128
SUBLANES = 8
GRID_W = 64
SSM_STATE = 128
SSM_CHUNK = 128
SSM_HALO = 16
SSM_CONV_COLS = 256
CONF_HALO = 16
CONF_ACC = 16
NEG_BIG = -1e30
BF16_ROWS = 16
NORM_CHUNKS = 4
PLE_CHUNKS = 2
NA_ROW_UNROLL = 8

V7X_VMEM_LIMIT = 56 * 1024 * 1024


def _cparams(sem):
    return pltpu.CompilerParams(dimension_semantics=sem, vmem_limit_bytes=V7X_VMEM_LIMIT)


def _sigmoid(x):
    return 1.0 / (1.0 + jnp.exp(-x))


def _silu(x):
    return x * _sigmoid(x)


def _rms_rows(x, w):
    ms = jnp.mean(x * x, axis=-1, keepdims=True)
    return x * lax.rsqrt(ms + EPS) * w


def _store_strips(o_ref, rows, val):
    for c in range(o_ref.shape[0]):
        o_ref[c, rows, :] = val[:, c * LANES:(c + 1) * LANES].astype(o_ref.dtype)


def _load_strips(x_ref):
    return jnp.concatenate([x_ref[c] for c in range(x_ref.shape[0])], axis=-1)


def _for_normed_rows(h_ref, nw_ref, xn_ref, emit):
    tm = h_ref.shape[0]
    rc = tm // NORM_CHUNKS

    @pl.when(pl.program_id(1) == 0)
    def _():
        for c in range(NORM_CHUNKS):
            rows = slice(c * rc, (c + 1) * rc)
            xn = _rms_rows(h_ref[rows, :], nw_ref[...]).astype(BF16)
            xn_ref[rows, :] = xn
            emit(rows, xn, True)

    @pl.when(pl.program_id(1) > 0)
    def _():
        emit(slice(0, tm), xn_ref[...], False)


def _norm_matmul_kernel(h_ref, nw_ref, w_ref, o_ref, xn_ref, *, strips, silu_from):
    def plain(rows, xn):
        acc = jnp.dot(xn, w_ref[...], preferred_element_type=F32)
        if strips:
            _store_strips(o_ref, rows, acc)
        else:
            o_ref[rows, :] = acc.astype(o_ref.dtype)

    def activated(rows, xn):
        rc = (rows.stop - rows.start) // NORM_CHUNKS
        for c in range(NORM_CHUNKS):
            acc = jnp.dot(xn[c * rc:(c + 1) * rc], w_ref[...], preferred_element_type=F32)
            sub = slice(rows.start + c * rc, rows.start + (c + 1) * rc)
            _store_strips(o_ref, sub, _silu(acc))

    def emit(rows, xn, first_step):
        if silu_from is None or first_step:
            plain(rows, xn)
        else:
            pl.when(pl.program_id(1) < silu_from)(lambda: plain(rows, xn))
            pl.when(pl.program_id(1) >= silu_from)(lambda: activated(rows, xn))

    _for_normed_rows(h_ref, nw_ref, xn_ref, emit)


def _norm_matmul(h, norm_w, w, out_dtype, *, strips=False, silu_from_col=None, tm=1024, tn=512):
    t, d = h.shape
    n = w.shape[1]
    tm = min(tm, t)
    tn = min(tn, n)
    assert t % tm == 0 and n % tn == 0 and tn % LANES == 0
    silu_from = None
    if silu_from_col is not None:
        assert strips and silu_from_col % tn == 0
        silu_from = silu_from_col // tn
    if strips:
        out_spec = pl.BlockSpec((tn // LANES, tm, LANES), lambda i, j: (j, i, 0))
        out_shape = jax.ShapeDtypeStruct((n // LANES, t, LANES), out_dtype)
    else:
        out_spec = pl.BlockSpec((tm, tn), lambda i, j: (i, j))
        out_shape = jax.ShapeDtypeStruct((t, n), out_dtype)
    return pl.pallas_call(
        functools.partial(_norm_matmul_kernel, strips=strips, silu_from=silu_from),
        grid=(t // tm, n // tn),
        in_specs=[pl.BlockSpec((tm, d), lambda i, j: (i, 0)),
                  pl.BlockSpec((1, d), lambda i, j: (0, 0)),
                  pl.BlockSpec((d, tn), lambda i, j: (0, j))],
        out_specs=out_spec,
        out_shape=out_shape,
        scratch_shapes=[pltpu.VMEM((tm, d), BF16)],
        compiler_params=_cparams(("parallel", "arbitrary")),
        name="norm_matmul_strips" if strips else "norm_matmul",
    )(h, norm_w.reshape(1, d), w)


def _conformer_in_kernel(h_ref, nw_ref, wa_ref, wb_ref, wg_ref, v_ref, g_ref, xn_ref):
    def emit(rows, xn, first_step):
        a = jnp.dot(xn, wa_ref[...], preferred_element_type=F32)
        a = a * _sigmoid(jnp.dot(xn, wb_ref[...], preferred_element_type=F32))
        _store_strips(v_ref, rows, a)
        g = _silu(jnp.dot(xn, wg_ref[...], preferred_element_type=F32))
        g_ref[rows, :] = g.astype(g_ref.dtype)

    _for_normed_rows(h_ref, nw_ref, xn_ref, emit)


def _conformer_in(h, norm_w, w, *, tm=1024, tn=512):
    t, d = h.shape
    n = w.shape[1] // 3
    tm = min(tm, t)
    tn = min(tn, n)
    assert t % tm == 0 and n % tn == 0 and tn % LANES == 0
    nb = n // tn

    def w_spec(section):
        return pl.BlockSpec((d, tn), lambda i, j: (0, section * nb + j))

    return pl.pallas_call(
        _conformer_in_kernel,
        grid=(t // tm, n // tn),
        in_specs=[pl.BlockSpec((tm, d), lambda i, j: (i, 0)),
                  pl.BlockSpec((1, d), lambda i, j: (0, 0)),
                  w_spec(0), w_spec(1), w_spec(2)],
        out_specs=[pl.BlockSpec((tn // LANES, tm, LANES), lambda i, j: (j, i, 0)),
                   pl.BlockSpec((tm, tn), lambda i, j: (i, j))],
        out_shape=[jax.ShapeDtypeStruct((n // LANES, t, LANES), BF16),
                   jax.ShapeDtypeStruct((t, n), BF16)],
        scratch_shapes=[pltpu.VMEM((tm, d), BF16)],
        compiler_params=_cparams(("parallel", "arbitrary")),
        name="conformer_in_proj",
    )(h, norm_w.reshape(1, d), w, w, w)


def _proj_residual_kernel(*refs, n_x):
    h_ref = refs[0]
    x_refs = refs[1:1 + n_x]
    w_refs = refs[1 + n_x:1 + 2 * n_x]
    o_ref = refs[1 + 2 * n_x]
    acc = h_ref[...]
    for x_ref, w_ref in zip(x_refs, w_refs):
        x = _load_strips(x_ref) if len(x_ref.shape) == 3 else x_ref[...]
        acc = acc + jnp.dot(x, w_ref[...], preferred_element_type=F32)
    o_ref[...] = acc


def _proj_residual(h, xs, w, *, tm=1024, tn=512):
    t, n = h.shape
    tm = min(tm, t)
    tn = min(tn, n)
    assert t % tm == 0 and n % tn == 0
    k = w.shape[0] // len(xs)
    in_specs = [pl.BlockSpec((tm, tn), lambda i, j: (i, j))]
    for x in xs:
        if x.ndim == 3:
            assert x.shape[0] * LANES == k
            in_specs.append(pl.BlockSpec((x.shape[0], tm, LANES), lambda i, j: (0, i, 0)))
        else:
            assert x.shape[1] == k
            in_specs.append(pl.BlockSpec((tm, k), lambda i, j: (i, 0)))
    in_specs += [pl.BlockSpec((k, tn), lambda i, j, r=r: (r, j)) for r in range(len(xs))]
    return pl.pallas_call(
        functools.partial(_proj_residual_kernel, n_x=len(xs)),
        grid=(t // tm, n // tn),
        in_specs=in_specs,
        out_specs=pl.BlockSpec((tm, tn), lambda i, j: (i, j)),
        out_shape=jax.ShapeDtypeStruct((t, n), F32),
        compiler_params=_cparams(("parallel", "arbitrary")),
        name="proj_residual",
    )(h, *xs, *([w] * len(xs)))


def _ple_kernel(h_ref, nw_ref, wg_ref, p_ref, wp_ref, *rest, final):
    if final:
        fw_ref, o_ref = rest
    else:
        (o_ref,) = rest
    rc = h_ref.shape[0] // PLE_CHUNKS
    for c in range(PLE_CHUNKS):
        rows = slice(c * rc, (c + 1) * rc)
        hc = h_ref[rows, :]
        xn = _rms_rows(hc, nw_ref[...]).astype(BF16)
        gate = _sigmoid(jnp.dot(xn, wg_ref[...], preferred_element_type=F32))
        proj = jnp.dot(p_ref[rows, :].astype(BF16), wp_ref[...], preferred_element_type=F32)
        out = hc + gate * proj
        if final:
            out = _rms_rows(out, fw_ref[...])
        o_ref[rows, :] = out


def _ple(h, norm_w, w_gate, p_all, layer, w_proj, final_w=None, *, tm=512):
    t, d = h.shape
    e = p_all.shape[2]
    tm = min(tm, t)
    assert t % tm == 0
    final = final_w is not None
    vec = pl.BlockSpec((1, d), lambda i: (0, 0))
    in_specs = [pl.BlockSpec((tm, d), lambda i: (i, 0)), vec,
                pl.BlockSpec((d, d), lambda i: (0, 0)),
                pl.BlockSpec((None, tm, e), lambda i: (layer, i, 0)),
                pl.BlockSpec((e, d), lambda i: (0, 0))]
    args = [h, norm_w.reshape(1, d), w_gate, p_all, w_proj]
    if final:
        in_specs.append(vec)
        args.append(final_w.reshape(1, d))
    return pl.pallas_call(
        functools.partial(_ple_kernel, final=final),
        grid=(t // tm,),
        in_specs=in_specs,
        out_specs=pl.BlockSpec((tm, d), lambda i: (i, 0)),
        out_shape=jax.ShapeDtypeStruct((t, d), F32),
        compiler_params=_cparams(("parallel",)),
        name="per_layer_embedding_final" if final else "per_layer_embedding",
    )(*args)


def _cast_kernel(w_ref, o_ref):
    o_ref[...] = w_ref[0].astype(o_ref.dtype)


def _cast_bf16(w_stack, layer, col0=0, ncols=None, *, tr=512, tc=2048):
    _, r, c = w_stack.shape
    ncols = c - col0 if ncols is None else ncols
    tr = min(tr, r)
    tc = math.gcd(tc, ncols)
    assert r % tr == 0 and col0 % tc == 0 and tc % LANES == 0
    cb = col0 // tc
    return pl.pallas_call(
        _cast_kernel,
        grid=(r // tr, ncols // tc),
        in_specs=[pl.BlockSpec((1, tr, tc), lambda i, j: (layer, i, cb + j))],
        out_specs=pl.BlockSpec((tr, tc), lambda i, j: (i, j)),
        out_shape=jax.ShapeDtypeStruct((r, ncols), BF16),
        compiler_params=_cparams(("parallel", "parallel")),
        name="cast_bf16",
    )(w_stack)


def _cast_shift_kernel(a_ref, b_ref, o_ref, *, shift, first_scale):
    a = a_ref[0]
    x = jnp.concatenate([a[:, shift:], b_ref[0][:, :shift]], axis=-1)
    x = jnp.where(pl.program_id(1) == 0, x * first_scale, x)
    o_ref[...] = x.astype(o_ref.dtype)


def _cast_bf16_unaligned(w_stack, layer, col0, ncols, first_scale, *, tr=256, tc=2048):
    _, r, c = w_stack.shape
    shift = col0 % LANES
    a0 = col0 - shift
    tr = min(tr, r)
    assert r % tr == 0 and ncols % tc == 0 and a0 % tc == 0 and 0 < shift < LANES
    assert a0 + ncols + shift <= c + LANES - 1
    return pl.pallas_call(
        functools.partial(_cast_shift_kernel, shift=shift, first_scale=first_scale),
        grid=(r // tr, ncols // tc),
        in_specs=[pl.BlockSpec((1, tr, tc), lambda i, j: (layer, i, a0 // tc + j)),
                  pl.BlockSpec((1, tr, LANES), lambda i, j: (layer, i, (a0 + tc * (j + 1)) // LANES))],
        out_specs=pl.BlockSpec((tr, tc), lambda i, j: (i, j)),
        out_shape=jax.ShapeDtypeStruct((r, ncols), BF16),
        compiler_params=_cparams(("parallel", "parallel")),
        name="cast_bf16_unaligned",
    )(w_stack, w_stack)


def _split_dot(v, e):
    hi = v.astype(BF16)
    lo = (v - hi.astype(F32)).astype(BF16)
    return (jnp.dot(hi, e, preferred_element_type=F32)
            + jnp.dot(lo, e, preferred_element_type=F32))


def _ssd_kernel(*refs, rev, final, n_chunks, d, heads, groups, conv_k):
    if final:
        (z_ref, xbc_ref, dt_ref, yother_ref,
         dtb_ref, alog_ref, dskip_ref, gw_ref, o_ref, state_ref) = refs
    else:
        (x_ref, bc_ref, xp_ref, bcp_ref, xn_ref, bcn_ref, dt_ref,
         cw_ref, cb_ref, dtb_ref, alog_ref, o_ref, xbc_out_ref, xpad_ref, state_ref) = refs

    q = SSM_CHUNK
    n = SSM_STATE
    hpg = heads // groups
    gw = d // groups
    p_dim = d // heads
    c = pl.program_id(1)
    ci = (n_chunks - 1 - c) if rev else c

    @pl.when(c == 0)
    def _():
        state_ref[...] = jnp.zeros_like(state_ref)

    if final:
        def conv_silu(lo, width):
            return xbc_ref[0, :, lo:lo + width]
    else:
        hl = SSM_HALO
        no_halo = jnp.zeros((hl, d), BF16)
        xpad_ref[0:hl, 0:d] = jnp.where(ci > 0, xp_ref[0], no_halo)
        xpad_ref[0:hl, d:2 * d] = jnp.where(ci > 0, bcp_ref[0], no_halo)
        xpad_ref[hl:hl + q, 0:d] = x_ref[0]
        xpad_ref[hl:hl + q, d:2 * d] = bc_ref[0]
        xpad_ref[hl + q:, 0:d] = jnp.where(ci < n_chunks - 1, xn_ref[0], no_halo)
        xpad_ref[hl + q:, d:2 * d] = jnp.where(ci < n_chunks - 1, bcn_ref[0], no_halo)

        half = conv_k // 2
        taps = [k for k in range(conv_k) if k != half]
        sel_row = lax.broadcasted_iota(jnp.int32, (q, q + 2 * hl), 0)
        sel_col = lax.broadcasted_iota(jnp.int32, (q, q + 2 * hl), 1)
        sel = jnp.concatenate([jnp.where(sel_col == sel_row + (hl + k - half), 1.0, 0.0).astype(BF16)
                               for k in taps], axis=0)
        for lo in range(0, 2 * d, SSM_CONV_COLS):
            csl = slice(lo, lo + SSM_CONV_COLS)
            xc = xpad_ref[:, csl]
            shifted = jnp.dot(sel, xc, preferred_element_type=F32)
            acc = cb_ref[:, csl] + cw_ref[half:half + 1, csl] * xc[hl:hl + q].astype(F32)
            for kk, k in enumerate(taps):
                acc = acc + cw_ref[k:k + 1, csl] * shifted[kk * q:(kk + 1) * q]
            xbc_out_ref[0, :, csl] = _silu(acc).astype(BF16)

        def conv_silu(lo, width):
            return xbc_out_ref[0, :, lo:lo + width]

    dsel = heads if rev else 0
    dt_raw = dt_ref[0][:, dsel:dsel + heads] + dtb_ref[...]
    dt = jnp.maximum(dt_raw, 0.0) + jnp.log1p(jnp.exp(-jnp.abs(dt_raw)))
    a = -jnp.exp(alog_ref[...])
    da = dt * a

    row = lax.broadcasted_iota(jnp.int32, (q, q), 0)
    col = lax.broadcasted_iota(jnp.int32, (q, q), 1)
    causal = (col >= row) if rev else (col <= row)
    tri = causal.astype(F32)
    cs_col = jnp.dot(tri, da, preferred_element_type=F32, precision=lax.Precision.HIGHEST)
    eye_h = (lax.broadcasted_iota(jnp.int32, (heads, heads), 0)
             == lax.broadcasted_iota(jnp.int32, (heads, heads), 1)).astype(F32)
    nt = (((1,), (1,)), ((), ()))
    cs_row = lax.dot_general(eye_h, cs_col, nt, preferred_element_type=F32,
                             precision=lax.Precision.HIGHEST)
    dt_row = lax.dot_general(eye_h, dt, nt, preferred_element_type=F32,
                             precision=lax.Precision.HIGHEST)
    last = 0 if rev else q - 1
    cs_last = cs_col[last:last + 1, :]

    e_lane = lax.broadcasted_iota(jnp.int32, (heads, d), 1)
    e_lo = lax.broadcasted_iota(jnp.int32, (heads, d), 0) * p_dim
    expand = jnp.where((e_lane >= e_lo) & (e_lane < e_lo + p_dim), 1.0, 0.0).astype(BF16)
    ecs_exp = _split_dot(jnp.exp(cs_col), expand)
    dec_exp = _split_dot(jnp.exp(cs_last - cs_col) * dt, expand)

    lane = lax.broadcasted_iota(jnp.int32, (1, gw), 1)

    for g in range(groups):
        xg_b = conv_silu(g * gw, gw)
        bg = conv_silu(d + g * n, n)
        cg = conv_silu(d + groups * n + g * n, n)
        gsl = slice(g * gw, (g + 1) * gw)
        xg = xg_b.astype(F32)

        scores = lax.dot_general(cg, bg, nt, preferred_element_type=F32)
        m_parts = []
        r_parts = []
        for hh in range(hpg):
            h = g * hpg + hh
            diff = cs_col[:, h:h + 1] - cs_row[h:h + 1, :]
            decay = jnp.exp(jnp.where(causal, diff, NEG_BIG))
            m_parts.append((scores * decay * dt_row[h:h + 1, :]).astype(BF16))
            in_head = (lane >= hh * p_dim) & (lane < (hh + 1) * p_dim)
            r_parts.append(jnp.where(in_head, xg_b, jnp.zeros_like(xg_b)))
        m_cat = jnp.concatenate(m_parts, axis=1)
        r_cat = jnp.concatenate(r_parts, axis=0)
        y = jnp.dot(m_cat, r_cat, preferred_element_type=F32)

        st = state_ref[g]
        y = y + jnp.dot(cg, st.astype(BF16), preferred_element_type=F32) * ecs_exp[:, gsl]
        contrib = lax.dot_general(bg, (xg * dec_exp[:, gsl]).astype(BF16),
                                  (((0,), (0,)), ((), ())), preferred_element_type=F32)
        state_ref[g] = st * ecs_exp[last:last + 1, gsl] + contrib

        if final:
            y = y + yother_ref[0][:, gsl] + dskip_ref[:, gsl] * xg
            y = y * _silu(z_ref[0][:, gsl].astype(F32))
            ms = jnp.mean(y * y, axis=-1, keepdims=True)
            y = y * lax.rsqrt(ms + EPS) * gw_ref[:, gsl]
        o_ref[0, :, gsl] = y.astype(o_ref.dtype)


def _ssd_specs(s, d, rev):
    q = SSM_CHUNK
    n_chunks = s // q
    hb = q // SSM_HALO
    n_hblk = s // SSM_HALO

    def cidx(c):
        return (n_chunks - 1 - c) if rev else c

    def main(colblk, width=d):
        return pl.BlockSpec((1, q, width), lambda bi, c: (bi, cidx(c), colblk))

    def prev(colblk):
        return pl.BlockSpec((1, SSM_HALO, d),
                            lambda bi, c: (bi, jnp.maximum(cidx(c) * hb - 1, 0), colblk))

    def nxt(colblk):
        return pl.BlockSpec((1, SSM_HALO, d),
                            lambda bi, c: (bi, jnp.minimum(cidx(c) * hb + hb, n_hblk - 1), colblk))

    def whole(arr):
        return pl.BlockSpec(arr.shape, lambda bi, c: (0,) * arr.ndim)

    return n_chunks, main, prev, nxt, whole


def _ssd_first(u, dt_raw, conv_w, conv_b, dt_bias, a_log, *, heads, groups, d):
    b, s, _ = u.shape
    n_chunks, main, prev, nxt, whole = _ssd_specs(s, d, rev=True)
    params = (conv_w, conv_b.reshape(1, -1), dt_bias.reshape(1, -1), a_log.reshape(1, -1))
    return pl.pallas_call(
        functools.partial(_ssd_kernel, rev=True, final=False, n_chunks=n_chunks, d=d,
                          heads=heads, groups=groups, conv_k=conv_w.shape[0]),
        grid=(b, n_chunks),
        in_specs=[main(1), main(2), prev(1), prev(2), nxt(1), nxt(2), main(0, dt_raw.shape[-1])]
                 + [whole(a) for a in params],
        out_specs=[main(0), main(0, 2 * d)],
        out_shape=[jax.ShapeDtypeStruct((b, s, d), F32),
                   jax.ShapeDtypeStruct((b, s, 2 * d), BF16)],
        scratch_shapes=[pltpu.VMEM((SSM_CHUNK + 2 * SSM_HALO, 2 * d), BF16),
                        pltpu.VMEM((groups, SSM_STATE, d // groups), F32)],
        compiler_params=_cparams(("parallel", "arbitrary")),
        name="ssd_first",
    )(u, u, u, u, u, u, dt_raw, *params)


def _ssd_final(u, xbc, dt_raw, y_other, dt_bias, a_log, d_skip_exp, gnorm_w, *, heads, groups, d):
    b, s, _ = u.shape
    n_chunks, main, _, _, whole = _ssd_specs(s, d, rev=False)
    params = (dt_bias.reshape(1, -1), a_log.reshape(1, -1), d_skip_exp.reshape(1, -1),
              gnorm_w.reshape(1, -1))
    return pl.pallas_call(
        functools.partial(_ssd_kernel, rev=False, final=True, n_chunks=n_chunks, d=d,
                          heads=heads, groups=groups, conv_k=0),
        grid=(b, n_chunks),
        in_specs=[main(0), main(0, 2 * d), main(0, dt_raw.shape[-1]), main(0)]
                 + [whole(a) for a in params],
        out_specs=main(0),
        out_shape=jax.ShapeDtypeStruct((b, s, d), BF16),
        scratch_shapes=[pltpu.VMEM((groups, SSM_STATE, d // groups), F32)],
        compiler_params=_cparams(("parallel", "arbitrary")),
        name="ssd_final",
    )(u, xbc, dt_raw, y_other, *params)


def _na_kernel(q_ref, k_ref, v_ref, g_ref, tbl_ref, o_ref, bias_ref, *, rows, kr, ro_base):
    w = GRID_W
    for dlt in range(kr):
        for i in range(0, kr, 2):
            pair = [tbl_ref[0, ro_base + dlt + i + j] for j in range(2)]
            bias_ref[dlt, :, i * w:(i + 2) * w] = jnp.concatenate(pair, axis=-1)

    def body(it, carry):
        rr = [it * NA_ROW_UNROLL + u for u in range(NA_ROW_UNROLL)]
        rs = [jnp.clip(r - kr // 2, 0, rows - kr) for r in rr]
        q0 = [pl.multiple_of(r * w, w) for r in rr]
        k0 = [pl.multiple_of(x * w, w) for x in rs]
        s = [lax.dot_general(q_ref[0, pl.ds(q0[u], w), :], k_ref[0, pl.ds(k0[u], kr * w), :],
                             (((1,), (1,)), ((), ())), preferred_element_type=F32)
             for u in range(NA_ROW_UNROLL)]
        s = [s[u] + bias_ref[rs[u] - rr[u] + kr - 1] for u in range(NA_ROW_UNROLL)]
        e = [jnp.exp(x - jnp.max(x, axis=-1, keepdims=True)) for x in s]
        l = [jnp.sum(x, axis=-1, keepdims=True) for x in e]
        o = [jnp.dot(e[u].astype(BF16), v_ref[0, pl.ds(k0[u], kr * w), :],
                     preferred_element_type=F32) for u in range(NA_ROW_UNROLL)]
        for u in range(NA_ROW_UNROLL):
            gate = g_ref[0, pl.ds(q0[u], w), :].astype(F32)
            o_ref[0, pl.ds(q0[u], w), :] = (o[u] / l[u] * gate).astype(o_ref.dtype)
        return carry

    lax.fori_loop(0, rows // NA_ROW_UNROLL, body, 0)


def _na_bias_table(rpb):
    win_cols = (rpb.shape[2] + 1) // 2
    cols = jnp.arange(GRID_W)
    col_start = jnp.clip(cols - win_cols // 2, 0, GRID_W - win_cols)
    col_valid = (cols[None, :] >= col_start[:, None]) & (cols[None, :] < col_start[:, None] + win_cols)
    col_off = jnp.clip(cols[None, :] - cols[:, None], -(win_cols - 1), win_cols - 1) + win_cols - 1
    onehot = (col_off[None] == jnp.arange(rpb.shape[2])[:, None, None]).astype(F32)
    tbl = jnp.einsum("hrc,cqk->hrqk", rpb.astype(F32), onehot, precision=lax.Precision.HIGHEST)
    return jnp.where(col_valid[None, None], tbl, NEG_BIG)


def _na_attention(u, rpb, *, b):
    n_sec, t, hd = u.shape
    heads = n_sec // 4
    s = t // b
    rows = s // GRID_W
    n_ro = rpb.shape[1]
    win_rows = (n_ro + 1) // 2
    kr = min(win_rows, rows)
    assert kr % 2 == 0 and 2 * GRID_W == LANES and rows % NA_ROW_UNROLL == 0
    tbl = _na_bias_table(rpb)

    def sec(i):
        return pl.BlockSpec((1, s, hd), lambda bi, h: (i * heads + h, bi, 0))

    return pl.pallas_call(
        functools.partial(_na_kernel, rows=rows, kr=kr, ro_base=win_rows - kr),
        grid=(b, heads),
        in_specs=[sec(0), sec(1), sec(2), sec(3),
                  pl.BlockSpec((1, n_ro, GRID_W, GRID_W), lambda bi, h: (h, 0, 0, 0))],
        out_specs=pl.BlockSpec((1, s, hd), lambda bi, h: (h, bi, 0)),
        out_shape=jax.ShapeDtypeStruct((heads, t, hd), BF16),
        scratch_shapes=[pltpu.VMEM((kr, GRID_W, kr * GRID_W), F32)],
        compiler_params=_cparams(("parallel", "arbitrary")),
        name="neighbourhood_attention",
    )(u, u, u, u, tbl)


def _conformer_kernel(v_ref, vp_ref, vn_ref, g_ref, w_ref, b_ref, lnw_ref, lnb_ref,
                      o_ref, xs_ref, conv_ref, *, n_blocks, width, rows):
    ci = pl.program_id(1)
    hl = CONF_HALO
    n_strips = v_ref.shape[0]
    seg = rows // SUBLANES
    p_in = seg + 2 * hl + SUBLANES
    p_out = seg + SUBLANES
    half = width // 2
    has_prev = (ci > 0).astype(F32)
    has_next = (ci < n_blocks - 1).astype(F32)

    for j in range(SUBLANES):
        base = p_in * j
        lo = seg * j - hl
        hi = seg * j + seg + hl
        if lo < 0:
            xs_ref[:, base:base + hl, :] = vp_ref[...].astype(F32) * has_prev
            xs_ref[:, base + hl:base + hl + hi, :] = v_ref[:, 0:hi, :].astype(F32)
        elif hi > rows:
            xs_ref[:, base:base + rows - lo, :] = v_ref[:, lo:rows, :].astype(F32)
            xs_ref[:, base + rows - lo:base + seg + 2 * hl, :] = vn_ref[...].astype(F32) * has_next
        else:
            xs_ref[:, base:base + seg + 2 * hl, :] = v_ref[:, lo:hi, :].astype(F32)

    off = hl - half

    def strip_body(s, carry):
        wv = [jnp.broadcast_to(w_ref[k, s], (SUBLANES, LANES)) for k in range(width)]
        bv = jnp.broadcast_to(b_ref[s], (SUBLANES, LANES))
        for i0 in range(0, seg, CONF_ACC):
            acc = [bv] * CONF_ACC
            for m in range(i0 + off, i0 + CONF_ACC - 1 + off + width):
                xv = xs_ref[s, pl.ds(m, SUBLANES, stride=p_in), :]
                for i in range(i0, i0 + CONF_ACC):
                    k = m - i - off
                    if 0 <= k < width:
                        acc[i - i0] = acc[i - i0] + wv[k] * xv
            for i in range(i0, i0 + CONF_ACC):
                conv_ref[s, pl.ds(i, SUBLANES, stride=p_out), :] = acc[i - i0]
        return carry

    lax.fori_loop(0, n_strips, strip_body, 0)

    inv_dc = 1.0 / (n_strips * LANES)

    def norm_body(j, carry):
        for rb in range(seg // BF16_ROWS):
            r_in = pl.multiple_of(j * p_out + rb * BF16_ROWS, SUBLANES)
            t0 = pl.multiple_of(j * seg + rb * BF16_ROWS, BF16_ROWS)
            x = conv_ref[:, pl.ds(r_in, BF16_ROWS), :]
            mu = jnp.sum(jnp.sum(x, axis=0), axis=-1, keepdims=True) * inv_dc
            xc = x - mu[None]
            var = jnp.sum(jnp.sum(xc * xc, axis=0), axis=-1, keepdims=True) * inv_dc
            y = _silu(xc * lax.rsqrt(var + EPS)[None] * lnw_ref[...] + lnb_ref[...])
            y = jnp.concatenate([y[s] for s in range(n_strips)], axis=-1).astype(BF16)
            o_ref[0, pl.ds(t0, BF16_ROWS), :] = y * g_ref[0, pl.ds(t0, BF16_ROWS), :]
        return carry

    lax.fori_loop(0, SUBLANES, norm_body, 0)


def _conformer_mix(v, g, dw_w, dw_b, ln_w, ln_b, *, b, rows=256):
    n_strips, t, _ = v.shape
    dc = n_strips * LANES
    s = t // b
    rows = min(rows, s)
    n_blocks = s // rows
    hb = rows // CONF_HALO
    n_hblk = s // CONF_HALO
    width = dw_w.shape[0]
    seg = rows // SUBLANES
    assert width // 2 <= CONF_HALO <= seg and seg % BF16_ROWS == 0 and seg % CONF_ACC == 0

    def whole(arr):
        return pl.BlockSpec(arr.shape, lambda bi, c: (0,) * arr.ndim)

    params = (dw_w.reshape(width, n_strips, 1, LANES), dw_b.reshape(n_strips, 1, LANES),
              ln_w.reshape(n_strips, 1, LANES), ln_b.reshape(n_strips, 1, LANES))
    return pl.pallas_call(
        functools.partial(_conformer_kernel, n_blocks=n_blocks, width=width, rows=rows),
        grid=(b, n_blocks),
        in_specs=[pl.BlockSpec((n_strips, rows, LANES), lambda bi, c: (0, bi * n_blocks + c, 0)),
                  pl.BlockSpec((n_strips, CONF_HALO, LANES),
                               lambda bi, c: (0, bi * n_hblk + jnp.maximum(c * hb - 1, 0), 0)),
                  pl.BlockSpec((n_strips, CONF_HALO, LANES),
                               lambda bi, c: (0, bi * n_hblk + jnp.minimum(c * hb + hb, n_hblk - 1), 0)),
                  pl.BlockSpec((1, rows, dc), lambda bi, c: (bi, c, 0))]
                 + [whole(a) for a in params],
        out_specs=pl.BlockSpec((1, rows, dc), lambda bi, c: (bi, c, 0)),
        out_shape=jax.ShapeDtypeStruct((b, s, dc), BF16),
        scratch_shapes=[pltpu.VMEM((n_strips, SUBLANES * (seg + 2 * CONF_HALO + SUBLANES), LANES), F32),
                        pltpu.VMEM((n_strips, SUBLANES * (seg + SUBLANES), LANES), F32)],
        compiler_params=_cparams(("parallel", "arbitrary")),
        name="conformer_conv_ln",
    )(v, v, v, g.reshape(b, s, dc), *params)


def _even_layer(h, b, s, e, norm_w, w_in_all, conv_w, conv_b, dt_bias_f, dt_bias_b, a_log_f, a_log_b,
                d_skip, gnorm_w, rpb, w_out_all):
    t, d = h.shape
    heads = d_skip.shape[0]
    conv_ch = conv_w.shape[1]
    groups = (conv_ch - d) // (2 * SSM_STATE)
    assert conv_ch == 2 * d, "SSD conv channels are fetched as two d-wide column blocks"
    na_heads = rpb.shape[0]
    n_dt = 2 * heads
    c_dt = d + conv_ch
    c_q = c_dt + n_dt

    w_ssd = _cast_bf16(w_in_all, e, 0, c_dt, tc=d)
    dt_pad = (-n_dt) % LANES
    w_dt = jnp.pad(w_in_all[e, :, c_dt:c_q], ((0, 0), (0, dt_pad))).astype(BF16)
    q_scale = (d // na_heads) ** -0.5
    w_na = _cast_bf16_unaligned(w_in_all, e, c_q, 4 * d, q_scale, tc=d)

    u = _norm_matmul(h, norm_w, w_ssd, BF16).reshape(b, s, -1)
    dt_raw = _norm_matmul(h, norm_w, w_dt, F32, tn=LANES).reshape(b, s, -1)
    assert d // na_heads == LANES, "attention heads are exchanged as 128-lane column strips"
    u_na = _norm_matmul(h, norm_w, w_na, BF16, strips=True, silu_from_col=3 * d)

    d_skip_exp = jnp.repeat(d_skip, d // heads)
    common = dict(heads=heads, groups=groups, d=d)
    y_b, xbc = _ssd_first(u, dt_raw, conv_w, conv_b, dt_bias_b, a_log_b, **common)
    y_ssd = _ssd_final(u, xbc, dt_raw, y_b, dt_bias_f, a_log_f, d_skip_exp, gnorm_w, **common)
    y_na = _na_attention(u_na, rpb, b=b)

    return _proj_residual(h, [y_ssd.reshape(t, d), y_na], _cast_bf16(w_out_all, e))


def _odd_layer(h, b, s, e, norm_w, w_in_all, dw_w, dw_b, ln_w, ln_b, w_out_all):
    v, g = _conformer_in(h, norm_w, _cast_bf16(w_in_all, e))
    y = _conformer_mix(v, g, dw_w, dw_b, ln_w, ln_b, b=b)
    return _proj_residual(h, [y.reshape(h.shape[0], -1)], _cast_bf16(w_out_all, e))


def kernel(x, p, ev_norm_w, ev_w_in, ev_conv_w, ev_conv_b, ev_dt_bias_f, ev_dt_bias_b, ev_a_log_f, ev_a_log_b, ev_d_skip, ev_gnorm_w, ev_rpb, ev_w_out, od_norm_w, od_w_in, od_dw_w, od_dw_b, od_ln_w, od_ln_b, od_w_out, ple_norm_w, ple_w_gate, ple_w_proj, final_norm_w):
    b, s, d = x.shape
    depth = p.shape[0]
    h = x.reshape(b * s, d)
    p_all = p.reshape(depth, b * s, -1)
    for i in range(depth):
        e = i // 2
        if i % 2 == 0:
            h = _even_layer(h, b, s, e, ev_norm_w[e], ev_w_in, ev_conv_w[e], ev_conv_b[e],
                            ev_dt_bias_f[e], ev_dt_bias_b[e], ev_a_log_f[e], ev_a_log_b[e],
                            ev_d_skip[e], ev_gnorm_w[e], ev_rpb[e], ev_w_out)
        else:
            h = _odd_layer(h, b, s, e, od_norm_w[e], od_w_in, od_dw_w[e], od_dw_b[e],
                           od_ln_w[e], od_ln_b[e], od_w_out)
        h = _ple(h, ple_norm_w[i], _cast_bf16(ple_w_gate, i), p_all, i,
                 _cast_bf16(ple_w_proj, i), final_norm_w if i == depth - 1 else None)
    return h.reshape(b, s, d)
```

```python
import functools
import math

import jax
import jax.numpy as jnp
from jax import lax
from jax.experimental import pallas as pl
from jax.experimental.pallas import tpu as pltpu

F32 = jnp.float32
BF16 = jnp.bfloat16

EPS = 1e-6
LANES = 128
SUBLANES = 8
GRID_W = 64
SSM_STATE = 128
SSM_CHUNK = 128
SSM_HALO = 16
SSM_CONV_COLS = 256
CONF_HALO = 16
CONF_ACC = 16
NEG_BIG = -1e30
BF16_ROWS = 16
NORM_CHUNKS = 4
PLE_CHUNKS = 2
NA_ROW_UNROLL = 8

V7X_VMEM_LIMIT = 56 * 1024 * 1024


def _cparams(sem):
    return pltpu.CompilerParams(dimension_semantics=sem, vmem_limit_bytes=V7X_VMEM_LIMIT)


def _sigmoid(x):
    return 1.0 / (1.0 + jnp.exp(-x))


def _silu(x):
    return x * _sigmoid(x)


def _rms_rows(x, w):
    ms = jnp.mean(x * x, axis=-1, keepdims=True)
    return x * lax.rsqrt(ms + EPS) * w


def _store_strips(o_ref, rows, val):
    for c in range(o_ref.shape[0]):
        o_ref[c, rows, :] = val[:, c * LANES:(c + 1) * LANES].astype(o_ref.dtype)


def _load_strips(x_ref):
    return jnp.concatenate([x_ref[c] for c in range(x_ref.shape[0])], axis=-1)


def _for_normed_rows(h_ref, nw_ref, xn_ref, emit):
    tm = h_ref.shape[0]
    rc = tm // NORM_CHUNKS

    @pl.when(pl.program_id(1) == 0)
    def _():
        for c in range(NORM_CHUNKS):
            rows = slice(c * rc, (c + 1) * rc)
            xn = _rms_rows(h_ref[rows, :], nw_ref[...]).astype(BF16)
            xn_ref[rows, :] = xn
            emit(rows, xn, True)

    @pl.when(pl.program_id(1) > 0)
    def _():
        emit(slice(0, tm), xn_ref[...], False)


def _norm_matmul_kernel(h_ref, nw_ref, w_ref, *rest, strips, silu_from, side):
    if side:
        w2_ref, o_ref, o2_ref, xn_ref = rest
    else:
        o_ref, xn_ref = rest

    def plain(rows, xn):
        acc = jnp.dot(xn, w_ref[...], preferred_element_type=F32)
        if strips:
            _store_strips(o_ref, rows, acc)
        else:
            o_ref[rows, :] = acc.astype(o_ref.dtype)

    def activated(rows, xn):
        rc = (rows.stop - rows.start) // NORM_CHUNKS
        for c in range(NORM_CHUNKS):
            acc = jnp.dot(xn[c * rc:(c + 1) * rc], w_ref[...], preferred_element_type=F32)
            sub = slice(rows.start + c * rc, rows.start + (c + 1) * rc)
            _store_strips(o_ref, sub, _silu(acc))

    def emit(rows, xn, first_step):
        if side and first_step:
            o2_ref[rows, :] = jnp.dot(xn, w2_ref[...].astype(BF16), preferred_element_type=F32)
        if silu_from is None or first_step:
            plain(rows, xn)
        else:
            pl.when(pl.program_id(1) < silu_from)(lambda: plain(rows, xn))
            pl.when(pl.program_id(1) >= silu_from)(lambda: activated(rows, xn))

    _for_normed_rows(h_ref, nw_ref, xn_ref, emit)


def _norm_matmul(h, norm_w, w, out_dtype, *, strips=False, silu_from_col=None, w_side=None,
                 tm=1024, tn=1024):
    t, d = h.shape
    n = w.shape[1]
    tm = min(tm, t)
    tn = math.gcd(tn, n)
    silu_from = None
    if silu_from_col is not None:
        assert strips and silu_from_col > 0
        tn = math.gcd(tn, silu_from_col)
        silu_from = silu_from_col // tn
    assert t % tm == 0 and tn % LANES == 0
    if strips:
        out_specs = [pl.BlockSpec((tn // LANES, tm, LANES), lambda i, j: (j, i, 0))]
        out_shape = [jax.ShapeDtypeStruct((n // LANES, t, LANES), out_dtype)]
    else:
        out_specs = [pl.BlockSpec((tm, tn), lambda i, j: (i, j))]
        out_shape = [jax.ShapeDtypeStruct((t, n), out_dtype)]
    in_specs = [pl.BlockSpec((tm, d), lambda i, j: (i, 0)),
                pl.BlockSpec((1, d), lambda i, j: (0, 0)),
                pl.BlockSpec((d, tn), lambda i, j: (0, j))]
    args = [h, norm_w.reshape(1, d), w]
    side = w_side is not None
    if side:
        ns = w_side.shape[1]
        in_specs.append(pl.BlockSpec((d, ns), lambda i, j: (0, 0)))
        args.append(w_side)
        out_specs.append(pl.BlockSpec((tm, ns), lambda i, j: (i, 0)))
        out_shape.append(jax.ShapeDtypeStruct((t, ns), F32))
    outs = pl.pallas_call(
        functools.partial(_norm_matmul_kernel, strips=strips, silu_from=silu_from, side=side),
        grid=(t // tm, n // tn),
        in_specs=in_specs,
        out_specs=out_specs,
        out_shape=out_shape,
        scratch_shapes=[pltpu.VMEM((tm, d), BF16)],
        compiler_params=_cparams(("parallel", "arbitrary")),
        name="norm_matmul_strips" if strips else "norm_matmul",
    )(*args)
    return outs if side else outs[0]


def _conformer_in_kernel(h_ref, nw_ref, wa_ref, wb_ref, wg_ref, v_ref, g_ref, xn_ref):
    def emit(rows, xn, first_step):
        a = jnp.dot(xn, wa_ref[...], preferred_element_type=F32)
        a = a * _sigmoid(jnp.dot(xn, wb_ref[...], preferred_element_type=F32))
        _store_strips(v_ref, rows, a)
        g = _silu(jnp.dot(xn, wg_ref[...], preferred_element_type=F32))
        g_ref[rows, :] = g.astype(g_ref.dtype)

    _for_normed_rows(h_ref, nw_ref, xn_ref, emit)


def _conformer_in(h, norm_w, w, *, tm=1024, tn=512):
    t, d = h.shape
    n = w.shape[1] // 3
    tm = min(tm, t)
    tn = min(tn, n)
    assert t % tm == 0 and n % tn == 0 and tn % LANES == 0
    nb = n // tn

    def w_spec(section):
        return pl.BlockSpec((d, tn), lambda i, j: (0, section * nb + j))

    return pl.pallas_call(
        _conformer_in_kernel,
        grid=(t // tm, n // tn),
        in_specs=[pl.BlockSpec((tm, d), lambda i, j: (i, 0)),
                  pl.BlockSpec((1, d), lambda i, j: (0, 0)),
                  w_spec(0), w_spec(1), w_spec(2)],
        out_specs=[pl.BlockSpec((tn // LANES, tm, LANES), lambda i, j: (j, i, 0)),
                   pl.BlockSpec((tm, tn), lambda i, j: (i, j))],
        out_shape=[jax.ShapeDtypeStruct((n // LANES, t, LANES), BF16),
                   jax.ShapeDtypeStruct((t, n), BF16)],
        scratch_shapes=[pltpu.VMEM((tm, d), BF16)],
        compiler_params=_cparams(("parallel", "arbitrary")),
        name="conformer_in_proj",
    )(h, norm_w.reshape(1, d), w, w, w)


def _proj_residual_kernel(*refs, n_x):
    h_ref = refs[0]
    x_refs = refs[1:1 + n_x]
    w_refs = refs[1 + n_x:1 + 2 * n_x]
    o_ref = refs[1 + 2 * n_x]
    acc = h_ref[...]
    for x_ref, w_ref in zip(x_refs, w_refs):
        x = _load_strips(x_ref) if len(x_ref.shape) == 3 else x_ref[...]
        acc = acc + jnp.dot(x, w_ref[...], preferred_element_type=F32)
    o_ref[...] = acc


def _proj_residual(h, xs, w, *, tm=1024, tn=512):
    t, n = h.shape
    tm = min(tm, t)
    tn = min(tn, n)
    assert t % tm == 0 and n % tn == 0
    k = w.shape[0] // len(xs)
    in_specs = [pl.BlockSpec((tm, tn), lambda i, j: (i, j))]
    for x in xs:
        if x.ndim == 3:
            assert x.shape[0] * LANES == k
            in_specs.append(pl.BlockSpec((x.shape[0], tm, LANES), lambda i, j: (0, i, 0)))
        else:
            assert x.shape[1] == k
            in_specs.append(pl.BlockSpec((tm, k), lambda i, j: (i, 0)))
    in_specs += [pl.BlockSpec((k, tn), lambda i, j, r=r: (r, j)) for r in range(len(xs))]
    return pl.pallas_call(
        functools.partial(_proj_residual_kernel, n_x=len(xs)),
        grid=(t // tm, n // tn),
        in_specs=in_specs,
        out_specs=pl.BlockSpec((tm, tn), lambda i, j: (i, j)),
        out_shape=jax.ShapeDtypeStruct((t, n), F32),
        compiler_params=_cparams(("parallel", "arbitrary")),
        name="proj_residual",
    )(h, *xs, *([w] * len(xs)))


def _ple_kernel(h_ref, nw_ref, wg_ref, p_ref, wp_ref, *rest, final):
    if final:
        fw_ref, o_ref = rest
    else:
        (o_ref,) = rest
    rc = h_ref.shape[0] // PLE_CHUNKS
    for c in range(PLE_CHUNKS):
        rows = slice(c * rc, (c + 1) * rc)
        hc = h_ref[rows, :]
        xn = _rms_rows(hc, nw_ref[...]).astype(BF16)
        gate = _sigmoid(jnp.dot(xn, wg_ref[...], preferred_element_type=F32))
        proj = jnp.dot(p_ref[rows, :].astype(BF16), wp_ref[...], preferred_element_type=F32)
        out = hc + gate * proj
        if final:
            out = _rms_rows(out, fw_ref[...])
        o_ref[rows, :] = out


def _ple(h, norm_w, w_gate, p_all, layer, w_proj, final_w=None, *, tm=512):
    t, d = h.shape
    e = p_all.shape[2]
    tm = min(tm, t)
    assert t % tm == 0
    final = final_w is not None
    vec = pl.BlockSpec((1, d), lambda i: (0, 0))
    in_specs = [pl.BlockSpec((tm, d), lambda i: (i, 0)), vec,
                pl.BlockSpec((d, d), lambda i: (0, 0)),
                pl.BlockSpec((None, tm, e), lambda i: (layer, i, 0)),
                pl.BlockSpec((e, d), lambda i: (0, 0))]
    args = [h, norm_w.reshape(1, d), w_gate, p_all, w_proj]
    if final:
        in_specs.append(vec)
        args.append(final_w.reshape(1, d))
    return pl.pallas_call(
        functools.partial(_ple_kernel, final=final),
        grid=(t // tm,),
        in_specs=in_specs,
        out_specs=pl.BlockSpec((tm, d), lambda i: (i, 0)),
        out_shape=jax.ShapeDtypeStruct((t, d), F32),
        compiler_params=_cparams(("parallel",)),
        name="per_layer_embedding_final" if final else "per_layer_embedding",
    )(*args)


def _cast_kernel(w_ref, o_ref):
    o_ref[...] = w_ref[0].astype(o_ref.dtype)


def _cast_bf16(w_stack, layer, col0=0, ncols=None, *, tr=512, tc=2048):
    _, r, c = w_stack.shape
    ncols = c - col0 if ncols is None else ncols
    tr = min(tr, r)
    tc = math.gcd(tc, ncols)
    assert r % tr == 0 and col0 % tc == 0 and tc % LANES == 0
    cb = col0 // tc
    return pl.pallas_call(
        _cast_kernel,
        grid=(r // tr, ncols // tc),
        in_specs=[pl.BlockSpec((1, tr, tc), lambda i, j: (layer, i, cb + j))],
        out_specs=pl.BlockSpec((tr, tc), lambda i, j: (i, j)),
        out_shape=jax.ShapeDtypeStruct((r, ncols), BF16),
        compiler_params=_cparams(("parallel", "parallel")),
        name="cast_bf16",
    )(w_stack)


def _cast_shift_kernel(a_ref, b_ref, o_ref, *, shift, first_scale):
    a = a_ref[0]
    x = jnp.concatenate([a[:, shift:], b_ref[0][:, :shift]], axis=-1)
    x = jnp.where(pl.program_id(1) == 0, x * first_scale, x)
    o_ref[...] = x.astype(o_ref.dtype)


def _cast_bf16_unaligned(w_stack, layer, col0, ncols, first_scale, *, tr=256, tc=2048):
    _, r, c = w_stack.shape
    shift = col0 % LANES
    a0 = col0 - shift
    tr = min(tr, r)
    assert r % tr == 0 and ncols % tc == 0 and a0 % tc == 0 and 0 < shift < LANES
    assert a0 + ncols + shift <= c + LANES - 1
    return pl.pallas_call(
        functools.partial(_cast_shift_kernel, shift=shift, first_scale=first_scale),
        grid=(r // tr, ncols // tc),
        in_specs=[pl.BlockSpec((1, tr, tc), lambda i, j: (layer, i, a0 // tc + j)),
                  pl.BlockSpec((1, tr, LANES), lambda i, j: (layer, i, (a0 + tc * (j + 1)) // LANES))],
        out_specs=pl.BlockSpec((tr, tc), lambda i, j: (i, j)),
        out_shape=jax.ShapeDtypeStruct((r, ncols), BF16),
        compiler_params=_cparams(("parallel", "parallel")),
        name="cast_bf16_unaligned",
    )(w_stack, w_stack)


def _split_dot(v, e):
    hi = v.astype(BF16)
    lo = (v - hi.astype(F32)).astype(BF16)
    return (jnp.dot(hi, e, preferred_element_type=F32)
            + jnp.dot(lo, e, preferred_element_type=F32))


def _ssd_kernel(*refs, rev, final, n_chunks, d, heads, groups, conv_k):
    if final:
        (z_ref, xbc_ref, dt_ref, yother_ref,
         dtb_ref, alog_ref, dskip_ref, gw_ref, o_ref, state_ref) = refs
    else:
        (x_ref, bc_ref, xp_ref, bcp_ref, xn_ref, bcn_ref, dt_ref,
         cw_ref, cb_ref, dtb_ref, alog_ref, o_ref, xbc_out_ref, xpad_ref, state_ref) = refs

    q = SSM_CHUNK
    n = SSM_STATE
    hpg = heads // groups
    gw = d // groups
    p_dim = d // heads
    c = pl.program_id(1)
    ci = (n_chunks - 1 - c) if rev else c

    @pl.when(c == 0)
    def _():
        state_ref[...] = jnp.zeros_like(state_ref)

    if final:
        def conv_silu(lo, width):
            return xbc_ref[0, :, lo:lo + width]
    else:
        hl = SSM_HALO
        no_halo = jnp.zeros((hl, d), BF16)
        xpad_ref[0:hl, 0:d] = jnp.where(ci > 0, xp_ref[0], no_halo)
        xpad_ref[0:hl, d:2 * d] = jnp.where(ci > 0, bcp_ref[0], no_halo)
        xpad_ref[hl:hl + q, 0:d] = x_ref[0]
        xpad_ref[hl:hl + q, d:2 * d] = bc_ref[0]
        xpad_ref[hl + q:, 0:d] = jnp.where(ci < n_chunks - 1, xn_ref[0], no_halo)
        xpad_ref[hl + q:, d:2 * d] = jnp.where(ci < n_chunks - 1, bcn_ref[0], no_halo)

        half = conv_k // 2
        taps = [k for k in range(conv_k) if k != half]
        sel_row = lax.broadcasted_iota(jnp.int32, (q, q + 2 * hl), 0)
        sel_col = lax.broadcasted_iota(jnp.int32, (q, q + 2 * hl), 1)
        sel = jnp.concatenate([jnp.where(sel_col == sel_row + (hl + k - half), 1.0, 0.0).astype(BF16)
                               for k in taps], axis=0)
        for lo in range(0, 2 * d, SSM_CONV_COLS):
            csl = slice(lo, lo + SSM_CONV_COLS)
            xc = xpad_ref[:, csl]
            shifted = jnp.dot(sel, xc, preferred_element_type=F32)
            acc = cb_ref[:, csl] + cw_ref[half:half + 1, csl] * xc[hl:hl + q].astype(F32)
            for kk, k in enumerate(taps):
                acc = acc + cw_ref[k:k + 1, csl] * shifted[kk * q:(kk + 1) * q]
            xbc_out_ref[0, :, csl] = _silu(acc).astype(BF16)

        def conv_silu(lo, width):
            return xbc_out_ref[0, :, lo:lo + width]

    dsel = heads if rev else 0
    dt_raw = dt_ref[0][:, dsel:dsel + heads] + dtb_ref[...]
    dt = jnp.maximum(dt_raw, 0.0) + jnp.log1p(jnp.exp(-jnp.abs(dt_raw)))
    a = -jnp.exp(alog_ref[...])
    da = dt * a

    row = lax.broadcasted_iota(jnp.int32, (q, q), 0)
    col = lax.broadcasted_iota(jnp.int32, (q, q), 1)
    causal = (col >= row) if rev else (col <= row)
    tri = causal.astype(F32)
    cs_col = jnp.dot(tri, da, preferred_element_type=F32, precision=lax.Precision.HIGHEST)
    eye_h = (lax.broadcasted_iota(jnp.int32, (heads, heads), 0)
             == lax.broadcasted_iota(jnp.int32, (heads, heads), 1)).astype(F32)
    nt = (((1,), (1,)), ((), ()))
    cs_row = lax.dot_general(eye_h, cs_col, nt, preferred_element_type=F32,
                             precision=lax.Precision.HIGHEST)
    dt_row = lax.dot_general(eye_h, dt, nt, preferred_element_type=F32,
                             precision=lax.Precision.HIGHEST)
    last = 0 if rev else q - 1
    cs_last = cs_col[last:last + 1, :]

    e_lane = lax.broadcasted_iota(jnp.int32, (heads, d), 1)
    e_lo = lax.broadcasted_iota(jnp.int32, (heads, d), 0) * p_dim
    expand = jnp.where((e_lane >= e_lo) & (e_lane < e_lo + p_dim), 1.0, 0.0).astype(BF16)
    ecs_exp = _split_dot(jnp.exp(cs_col), expand)
    dec_exp = _split_dot(jnp.exp(cs_last - cs_col) * dt, expand)

    lane = lax.broadcasted_iota(jnp.int32, (1, gw), 1)

    for g in range(groups):
        xg_b = conv_silu(g * gw, gw)
        bg = conv_silu(d + g * n, n)
        cg = conv_silu(d + groups * n + g * n, n)
        gsl = slice(g * gw, (g + 1) * gw)
        xg = xg_b.astype(F32)

        scores = lax.dot_general(cg, bg, nt, preferred_element_type=F32)
        m_parts = []
        r_parts = []
        for hh in range(hpg):
            h = g * hpg + hh
            diff = cs_col[:, h:h + 1] - cs_row[h:h + 1, :]
            decay = jnp.exp(jnp.where(causal, diff, NEG_BIG))
            m_parts.append((scores * decay * dt_row[h:h + 1, :]).astype(BF16))
            in_head = (lane >= hh * p_dim) & (lane < (hh + 1) * p_dim)
            r_parts.append(jnp.where(in_head, xg_b, jnp.zeros_like(xg_b)))
        m_cat = jnp.concatenate(m_parts, axis=1)
        r_cat = jnp.concatenate(r_parts, axis=0)
        y = jnp.dot(m_cat, r_cat, preferred_element_type=F32)

        st = state_ref[g]
        y = y + jnp.dot(cg, st.astype(BF16), preferred_element_type=F32) * ecs_exp[:, gsl]
        contrib = lax.dot_general(bg, (xg * dec_exp[:, gsl]).astype(BF16),
                                  (((0,), (0,)), ((), ())), preferred_element_type=F32)
        state_ref[g] = st * ecs_exp[last:last + 1, gsl] + contrib

        if final:
            y = y + yother_ref[0][:, gsl] + dskip_ref[:, gsl] * xg
            y = y * _silu(z_ref[0][:, gsl].astype(F32))
            ms = jnp.mean(y * y, axis=-1, keepdims=True)
            y = y * lax.rsqrt(ms + EPS) * gw_ref[:, gsl]
        o_ref[0, :, gsl] = y.astype(o_ref.dtype)


def _ssd_specs(s, d, rev):
    q = SSM_CHUNK
    n_chunks = s // q
    hb = q // SSM_HALO
    n_hblk = s // SSM_HALO

    def cidx(c):
        return (n_chunks - 1 - c) if rev else c

    def main(colblk, width=d):
        return pl.BlockSpec((1, q, width), lambda bi, c: (bi, cidx(c), colblk))

    def prev(colblk):
        return pl.BlockSpec((1, SSM_HALO, d),
                            lambda bi, c: (bi, jnp.maximum(cidx(c) * hb - 1, 0), colblk))

    def nxt(colblk):
        return pl.BlockSpec((1, SSM_HALO, d),
                            lambda bi, c: (bi, jnp.minimum(cidx(c) * hb + hb, n_hblk - 1), colblk))

    def whole(arr):
        return pl.BlockSpec(arr.shape, lambda bi, c: (0,) * arr.ndim)

    return n_chunks, main, prev, nxt, whole


def _ssd_first(u, dt_raw, conv_w, conv_b, dt_bias, a_log, *, heads, groups, d):
    b, s, _ = u.shape
    n_chunks, main, prev, nxt, whole = _ssd_specs(s, d, rev=True)
    params = (conv_w, conv_b.reshape(1, -1), dt_bias.reshape(1, -1), a_log.reshape(1, -1))
    return pl.pallas_call(
        functools.partial(_ssd_kernel, rev=True, final=False, n_chunks=n_chunks, d=d,
                          heads=heads, groups=groups, conv_k=conv_w.shape[0]),
        grid=(b, n_chunks),
        in_specs=[main(1), main(2), prev(1), prev(2), nxt(1), nxt(2), main(0, dt_raw.shape[-1])]
                 + [whole(a) for a in params],
        out_specs=[main(0), main(0, 2 * d)],
        out_shape=[jax.ShapeDtypeStruct((b, s, d), F32),
                   jax.ShapeDtypeStruct((b, s, 2 * d), BF16)],
        scratch_shapes=[pltpu.VMEM((SSM_CHUNK + 2 * SSM_HALO, 2 * d), BF16),
                        pltpu.VMEM((groups, SSM_STATE, d // groups), F32)],
        compiler_params=_cparams(("parallel", "arbitrary")),
        name="ssd_first",
    )(u, u, u, u, u, u, dt_raw, *params)


def _ssd_final(u, xbc, dt_raw, y_other, dt_bias, a_log, d_skip_exp, gnorm_w, *, heads, groups, d):
    b, s, _ = u.shape
    n_chunks, main, _, _, whole = _ssd_specs(s, d, rev=False)
    params = (dt_bias.reshape(1, -1), a_log.reshape(1, -1), d_skip_exp.reshape(1, -1),
              gnorm_w.reshape(1, -1))
    return pl.pallas_call(
        functools.partial(_ssd_kernel, rev=False, final=True, n_chunks=n_chunks, d=d,
                          heads=heads, groups=groups, conv_k=0),
        grid=(b, n_chunks),
        in_specs=[main(0), main(0, 2 * d), main(0, dt_raw.shape[-1]), main(0)]
                 + [whole(a) for a in params],
        out_specs=main(0),
        out_shape=jax.ShapeDtypeStruct((b, s, d), BF16),
        scratch_shapes=[pltpu.VMEM((groups, SSM_STATE, d // groups), F32)],
        compiler_params=_cparams(("parallel", "arbitrary")),
        name="ssd_final",
    )(u, xbc, dt_raw, y_other, *params)


def _na_kernel(q_ref, k_ref, v_ref, g_ref, tbl_ref, o_ref, bias_ref, *, rows, kr, ro_base):
    w = GRID_W
    for dlt in range(kr):
        for i in range(0, kr, 2):
            pair = [tbl_ref[0, ro_base + dlt + i + j] for j in range(2)]
            bias_ref[dlt, :, i * w:(i + 2) * w] = jnp.concatenate(pair, axis=-1)

    def body(it, carry):
        rr = [it * NA_ROW_UNROLL + u for u in range(NA_ROW_UNROLL)]
        rs = [jnp.clip(r - kr // 2, 0, rows - kr) for r in rr]
        q0 = [pl.multiple_of(r * w, w) for r in rr]
        k0 = [pl.multiple_of(x * w, w) for x in rs]
        s = [lax.dot_general(q_ref[0, pl.ds(q0[u], w), :], k_ref[0, pl.ds(k0[u], kr * w), :],
                             (((1,), (1,)), ((), ())), preferred_element_type=F32)
             for u in range(NA_ROW_UNROLL)]
        s = [s[u] + bias_ref[rs[u] - rr[u] + kr - 1] for u in range(NA_ROW_UNROLL)]
        e = [jnp.exp(x - jnp.max(x, axis=-1, keepdims=True)) for x in s]
        l = [jnp.sum(x, axis=-1, keepdims=True) for x in e]
        o = [jnp.dot(e[u].astype(BF16), v_ref[0, pl.ds(k0[u], kr * w), :],
                     preferred_element_type=F32) for u in range(NA_ROW_UNROLL)]
        for u in range(NA_ROW_UNROLL):
            gate = g_ref[0, pl.ds(q0[u], w), :].astype(F32)
            o_ref[0, pl.ds(q0[u], w), :] = (o[u] / l[u] * gate).astype(o_ref.dtype)
        return carry

    lax.fori_loop(0, rows // NA_ROW_UNROLL, body, 0)


def _na_bias_table(rpb):
    win_cols = (rpb.shape[2] + 1) // 2
    cols = jnp.arange(GRID_W)
    col_start = jnp.clip(cols - win_cols // 2, 0, GRID_W - win_cols)
    col_valid = (cols[None, :] >= col_start[:, None]) & (cols[None, :] < col_start[:, None] + win_cols)
    col_off = jnp.clip(cols[None, :] - cols[:, None], -(win_cols - 1), win_cols - 1) + win_cols - 1
    onehot = (col_off[None] == jnp.arange(rpb.shape[2])[:, None, None]).astype(F32)
    tbl = jnp.einsum("hrc,cqk->hrqk", rpb.astype(F32), onehot, precision=lax.Precision.HIGHEST)
    return jnp.where(col_valid[None, None], tbl, NEG_BIG)


def _na_attention(u, rpb, *, b):
    n_sec, t, hd = u.shape
    heads = n_sec // 4
    s = t // b
    rows = s // GRID_W
    n_ro = rpb.shape[1]
    win_rows = (n_ro + 1) // 2
    kr = min(win_rows, rows)
    assert kr % 2 == 0 and 2 * GRID_W == LANES and rows % NA_ROW_UNROLL == 0
    tbl = _na_bias_table(rpb)

    def sec(i):
        return pl.BlockSpec((1, s, hd), lambda bi, h: (i * heads + h, bi, 0))

    return pl.pallas_call(
        functools.partial(_na_kernel, rows=rows, kr=kr, ro_base=win_rows - kr),
        grid=(b, heads),
        in_specs=[sec(0), sec(1), sec(2), sec(3),
                  pl.BlockSpec((1, n_ro, GRID_W, GRID_W), lambda bi, h: (h, 0, 0, 0))],
        out_specs=pl.BlockSpec((1, s, hd), lambda bi, h: (h, bi, 0)),
        out_shape=jax.ShapeDtypeStruct((heads, t, hd), BF16),
        scratch_shapes=[pltpu.VMEM((kr, GRID_W, kr * GRID_W), F32)],
        compiler_params=_cparams(("parallel", "arbitrary")),
        name="neighbourhood_attention",
    )(u, u, u, u, tbl)


def _conformer_kernel(v_ref, vp_ref, vn_ref, g_ref, w_ref, b_ref, lnw_ref, lnb_ref,
                      o_ref, xs_ref, conv_ref, *, n_blocks, width, rows):
    ci = pl.program_id(1)
    hl = CONF_HALO
    n_strips = v_ref.shape[0]
    seg = rows // SUBLANES
    p_in = seg + 2 * hl + SUBLANES
    p_out = seg + SUBLANES
    half = width // 2
    has_prev = (ci > 0).astype(F32)
    has_next = (ci < n_blocks - 1).astype(F32)

    for j in range(SUBLANES):
        base = p_in * j
        lo = seg * j - hl
        hi = seg * j + seg + hl
        if lo < 0:
            xs_ref[:, base:base + hl, :] = vp_ref[...].astype(F32) * has_prev
            xs_ref[:, base + hl:base + hl + hi, :] = v_ref[:, 0:hi, :].astype(F32)
        elif hi > rows:
            xs_ref[:, base:base + rows - lo, :] = v_ref[:, lo:rows, :].astype(F32)
            xs_ref[:, base + rows - lo:base + seg + 2 * hl, :] = vn_ref[...].astype(F32) * has_next
        else:
            xs_ref[:, base:base + seg + 2 * hl, :] = v_ref[:, lo:hi, :].astype(F32)

    off = hl - half

    def strip_body(s, carry):
        wv = [jnp.broadcast_to(w_ref[k, s], (SUBLANES, LANES)) for k in range(width)]
        bv = jnp.broadcast_to(b_ref[s], (SUBLANES, LANES))
        for i0 in range(0, seg, CONF_ACC):
            acc = [bv] * CONF_ACC
            for m in range(i0 + off, i0 + CONF_ACC - 1 + off + width):
                xv = xs_ref[s, pl.ds(m, SUBLANES, stride=p_in), :]
                for i in range(i0, i0 + CONF_ACC):
                    k = m - i - off
                    if 0 <= k < width:
                        acc[i - i0] = acc[i - i0] + wv[k] * xv
            for i in range(i0, i0 + CONF_ACC):
                conv_ref[s, pl.ds(i, SUBLANES, stride=p_out), :] = acc[i - i0]
        return carry

    lax.fori_loop(0, n_strips, strip_body, 0)

    inv_dc = 1.0 / (n_strips * LANES)

    def norm_body(j, carry):
        for rb in range(seg // BF16_ROWS):
            r_in = pl.multiple_of(j * p_out + rb * BF16_ROWS, SUBLANES)
            t0 = pl.multiple_of(j * seg + rb * BF16_ROWS, BF16_ROWS)
            x = conv_ref[:, pl.ds(r_in, BF16_ROWS), :]
            mu = jnp.sum(jnp.sum(x, axis=0), axis=-1, keepdims=True) * inv_dc
            xc = x - mu[None]
            var = jnp.sum(jnp.sum(xc * xc, axis=0), axis=-1, keepdims=True) * inv_dc
            y = _silu(xc * lax.rsqrt(var + EPS)[None] * lnw_ref[...] + lnb_ref[...])
            y = jnp.concatenate([y[s] for s in range(n_strips)], axis=-1).astype(BF16)
            o_ref[0, pl.ds(t0, BF16_ROWS), :] = y * g_ref[0, pl.ds(t0, BF16_ROWS), :]
        return carry

    lax.fori_loop(0, SUBLANES, norm_body, 0)


def _conformer_mix(v, g, dw_w, dw_b, ln_w, ln_b, *, b, rows=256):
    n_strips, t, _ = v.shape
    dc = n_strips * LANES
    s = t // b
    rows = min(rows, s)
    n_blocks = s // rows
    hb = rows // CONF_HALO
    n_hblk = s // CONF_HALO
    width = dw_w.shape[0]
    seg = rows // SUBLANES
    assert width // 2 <= CONF_HALO <= seg and seg % BF16_ROWS == 0 and seg % CONF_ACC == 0

    def whole(arr):
        return pl.BlockSpec(arr.shape, lambda bi, c: (0,) * arr.ndim)

    params = (dw_w.reshape(width, n_strips, 1, LANES), dw_b.reshape(n_strips, 1, LANES),
              ln_w.reshape(n_strips, 1, LANES), ln_b.reshape(n_strips, 1, LANES))
    return pl.pallas_call(
        functools.partial(_conformer_kernel, n_blocks=n_blocks, width=width, rows=rows),
        grid=(b, n_blocks),
        in_specs=[pl.BlockSpec((n_strips, rows, LANES), lambda bi, c: (0, bi * n_blocks + c, 0)),
                  pl.BlockSpec((n_strips, CONF_HALO, LANES),
                               lambda bi, c: (0, bi * n_hblk + jnp.maximum(c * hb - 1, 0), 0)),
                  pl.BlockSpec((n_strips, CONF_HALO, LANES),
                               lambda bi, c: (0, bi * n_hblk + jnp.minimum(c * hb + hb, n_hblk - 1), 0)),
                  pl.BlockSpec((1, rows, dc), lambda bi, c: (bi, c, 0))]
                 + [whole(a) for a in params],
        out_specs=pl.BlockSpec((1, rows, dc), lambda bi, c: (bi, c, 0)),
        out_shape=jax.ShapeDtypeStruct((b, s, dc), BF16),
        scratch_shapes=[pltpu.VMEM((n_strips, SUBLANES * (seg + 2 * CONF_HALO + SUBLANES), LANES), F32),
                        pltpu.VMEM((n_strips, SUBLANES * (seg + SUBLANES), LANES), F32)],
        compiler_params=_cparams(("parallel", "arbitrary")),
        name="conformer_conv_ln",
    )(v, v, v, g.reshape(b, s, dc), *params)


def _even_layer(h, b, s, e, norm_w, w_in_all, conv_w, conv_b, dt_bias_f, dt_bias_b, a_log_f, a_log_b,
                d_skip, gnorm_w, rpb, w_out_all):
    t, d = h.shape
    heads = d_skip.shape[0]
    conv_ch = conv_w.shape[1]
    groups = (conv_ch - d) // (2 * SSM_STATE)
    assert conv_ch == 2 * d, "SSD conv channels are fetched as two d-wide column blocks"
    na_heads = rpb.shape[0]
    n_dt = 2 * heads
    c_dt = d + conv_ch
    c_q = c_dt + n_dt

    w_ssd = _cast_bf16(w_in_all, e, 0, c_dt, tc=d)
    dt_pad = (-n_dt) % LANES
    w_dt = jnp.pad(w_in_all[e, :, c_dt:c_q], ((0, 0), (0, dt_pad)))
    q_scale = (d // na_heads) ** -0.5
    w_na = _cast_bf16_unaligned(w_in_all, e, c_q, 4 * d, q_scale, tc=d)

    u, dt_raw = _norm_matmul(h, norm_w, w_ssd, BF16, w_side=w_dt)
    u = u.reshape(b, s, -1)
    dt_raw = dt_raw.reshape(b, s, -1)
    assert d // na_heads == LANES, "attention heads are exchanged as 128-lane column strips"
    u_na = _norm_matmul(h, norm_w, w_na, BF16, strips=True, silu_from_col=3 * d)

    d_skip_exp = jnp.repeat(d_skip, d // heads)
    common = dict(heads=heads, groups=groups, d=d)
    y_b, xbc = _ssd_first(u, dt_raw, conv_w, conv_b, dt_bias_b, a_log_b, **common)
    y_ssd = _ssd_final(u, xbc, dt_raw, y_b, dt_bias_f, a_log_f, d_skip_exp, gnorm_w, **common)
    y_na = _na_attention(u_na, rpb, b=b)

    return _proj_residual(h, [y_ssd.reshape(t, d), y_na], _cast_bf16(w_out_all, e))


def _odd_layer(h, b, s, e, norm_w, w_in_all, dw_w, dw_b, ln_w, ln_b, w_out_all):
    v, g = _conformer_in(h, norm_w, _cast_bf16(w_in_all, e))
    y = _conformer_mix(v, g, dw_w, dw_b, ln_w, ln_b, b=b)
    return _proj_residual(h, [y.reshape(h.shape[0], -1)], _cast_bf16(w_out_all, e))


def kernel(x, p, ev_norm_w, ev_w_in, ev_conv_w, ev_conv_b, ev_dt_bias_f, ev_dt_bias_b, ev_a_log_f, ev_a_log_b, ev_d_skip, ev_gnorm_w, ev_rpb, ev_w_out, od_norm_w, od_w_in, od_dw_w, od_dw_b, od_ln_w, od_ln_b, od_w_out, ple_norm_w, ple_w_gate, ple_w_proj, final_norm_w):
    b, s, d = x.shape
    depth = p.shape[0]
    h = x.reshape(b * s, d)
    p_all = p.reshape(depth, b * s, -1)
    for i in range(depth):
        e = i // 2
        if i % 2 == 0:
            h = _even_layer(h, b, s, e, ev_norm_w[e], ev_w_in, ev_conv_w[e], ev_conv_b[e],
                            ev_dt_bias_f[e], ev_dt_bias_b[e], ev_a_log_f[e], ev_a_log_b[e],
                            ev_d_skip[e], ev_gnorm_w[e], ev_rpb[e], ev_w_out)
        else:
            h = _odd_layer(h, b, s, e, od_norm_w[e], od_w_in, od_dw_w[e], od_dw_b[e],
                           od_ln_w[e], od_ln_b[e], od_w_out)
        h = _ple(h, ple_norm_w[i], _cast_bf16(ple_w_gate, i), p_all, i,
                 _cast_bf16(ple_w_proj, i), final_norm_w if i == depth - 1 else None)
    return h.reshape(b, s, d)
```

```python
import functools
import math

import jax
import jax.numpy as jnp
from jax import lax
from jax.experimental import pallas as pl
from jax.experimental.pallas import tpu as pltpu

F32 = jnp.float32
BF16 = jnp.bfloat16

EPS = 1e-6
LANES = 128
SUBLANES = 8
GRID_W = 64
SSM_STATE = 128
SSM_CHUNK = 128
SSM_HALO = 16
SSM_CONV_COLS = 256
CONF_HALO = 16
CONF_ACC = 16
NEG_BIG = -1e30
BF16_ROWS = 16
NORM_CHUNKS = 4
PLE_CHUNKS = 2
NA_ROW_UNROLL = 8

V7X_VMEM_LIMIT = 56 * 1024 * 1024

PROJ_ROWS = 1024
NORM_PROJ_COLS = 1024
CONF_PROJ_COLS = 512
OUT_PROJ_COLS = 1024
PLE_ROWS = 512
CAST_ROWS = 512
CAST_COLS = 2048
CONF_ROWS = 256


def _cparams(sem):
    return pltpu.CompilerParams(dimension_semantics=sem, vmem_limit_bytes=V7X_VMEM_LIMIT)


def _sigmoid(x):
    return 1.0 / (1.0 + jnp.exp(-x))


def _silu(x):
    return x * _sigmoid(x)


def _rms_rows(x, w):
    ms = jnp.mean(x * x, axis=-1, keepdims=True)
    return x * lax.rsqrt(ms + EPS) * w


def _store_strips(o_ref, rows, val):
    for c in range(o_ref.shape[0]):
        o_ref[c, rows, :] = val[:, c * LANES:(c + 1) * LANES].astype(o_ref.dtype)


def _load_strips(x_ref):
    return jnp.concatenate([x_ref[c] for c in range(x_ref.shape[0])], axis=-1)


def _for_normed_rows(h_ref, nw_ref, xn_ref, emit):
    tm = h_ref.shape[0]
    rc = tm // NORM_CHUNKS

    @pl.when(pl.program_id(1) == 0)
    def _():
        for c in range(NORM_CHUNKS):
            rows = slice(c * rc, (c + 1) * rc)
            xn = _rms_rows(h_ref[rows, :], nw_ref[...]).astype(BF16)
            xn_ref[rows, :] = xn
            emit(rows, xn, True)

    @pl.when(pl.program_id(1) > 0)
    def _():
        emit(slice(0, tm), xn_ref[...], False)


def _norm_matmul_kernel(h_ref, nw_ref, w_ref, *rest, strips, silu_from, side):
    if side:
        w2_ref, o_ref, o2_ref, xn_ref = rest
    else:
        o_ref, xn_ref = rest

    def plain(rows, xn):
        acc = jnp.dot(xn, w_ref[...], preferred_element_type=F32)
        if strips:
            _store_strips(o_ref, rows, acc)
        else:
            o_ref[rows, :] = acc.astype(o_ref.dtype)

    def activated(rows, xn):
        rc = (rows.stop - rows.start) // NORM_CHUNKS
        for c in range(NORM_CHUNKS):
            acc = jnp.dot(xn[c * rc:(c + 1) * rc], w_ref[...], preferred_element_type=F32)
            sub = slice(rows.start + c * rc, rows.start + (c + 1) * rc)
            _store_strips(o_ref, sub, _silu(acc))

    def emit(rows, xn, first_step):
        if side and first_step:
            o2_ref[rows, :] = jnp.dot(xn, w2_ref[...].astype(BF16), preferred_element_type=F32)
        if silu_from is None or first_step:
            plain(rows, xn)
        else:
            pl.when(pl.program_id(1) < silu_from)(lambda: plain(rows, xn))
            pl.when(pl.program_id(1) >= silu_from)(lambda: activated(rows, xn))

    _for_normed_rows(h_ref, nw_ref, xn_ref, emit)


def _norm_matmul(h, norm_w, w, out_dtype, *, strips=False, silu_from_col=None, w_side=None,
                 tm=PROJ_ROWS, tn=NORM_PROJ_COLS):
    t, d = h.shape
    n = w.shape[1]
    tm = min(tm, t)
    tn = math.gcd(tn, n)
    silu_from = None
    if silu_from_col is not None:
        assert strips and silu_from_col > 0
        tn = math.gcd(tn, silu_from_col)
        silu_from = silu_from_col // tn
    assert t % tm == 0 and tn % LANES == 0
    if strips:
        out_specs = [pl.BlockSpec((tn // LANES, tm, LANES), lambda i, j: (j, i, 0))]
        out_shape = [jax.ShapeDtypeStruct((n // LANES, t, LANES), out_dtype)]
    else:
        out_specs = [pl.BlockSpec((tm, tn), lambda i, j: (i, j))]
        out_shape = [jax.ShapeDtypeStruct((t, n), out_dtype)]
    in_specs = [pl.BlockSpec((tm, d), lambda i, j: (i, 0)),
                pl.BlockSpec((1, d), lambda i, j: (0, 0)),
                pl.BlockSpec((d, tn), lambda i, j: (0, j))]
    args = [h, norm_w.reshape(1, d), w]
    side = w_side is not None
    if side:
        ns = w_side.shape[1]
        in_specs.append(pl.BlockSpec((d, ns), lambda i, j: (0, 0)))
        args.append(w_side)
        out_specs.append(pl.BlockSpec((tm, ns), lambda i, j: (i, 0)))
        out_shape.append(jax.ShapeDtypeStruct((t, ns), F32))
    outs = pl.pallas_call(
        functools.partial(_norm_matmul_kernel, strips=strips, silu_from=silu_from, side=side),
        grid=(t // tm, n // tn),
        in_specs=in_specs,
        out_specs=out_specs,
        out_shape=out_shape,
        scratch_shapes=[pltpu.VMEM((tm, d), BF16)],
        compiler_params=_cparams(("parallel", "arbitrary")),
        name="norm_matmul_strips" if strips else "norm_matmul",
    )(*args)
    return outs if side else outs[0]


def _conformer_in_kernel(h_ref, nw_ref, wa_ref, wb_ref, wg_ref, v_ref, g_ref, xn_ref):
    def emit(rows, xn, first_step):
        a = jnp.dot(xn, wa_ref[...], preferred_element_type=F32)
        a = a * _sigmoid(jnp.dot(xn, wb_ref[...], preferred_element_type=F32))
        _store_strips(v_ref, rows, a)
        g = _silu(jnp.dot(xn, wg_ref[...], preferred_element_type=F32))
        g_ref[rows, :] = g.astype(g_ref.dtype)

    _for_normed_rows(h_ref, nw_ref, xn_ref, emit)


def _conformer_in(h, norm_w, w, *, tm=PROJ_ROWS, tn=CONF_PROJ_COLS):
    t, d = h.shape
    n = w.shape[1] // 3
    tm = min(tm, t)
    tn = min(tn, n)
    assert t % tm == 0 and n % tn == 0 and tn % LANES == 0
    nb = n // tn

    def w_spec(section):
        return pl.BlockSpec((d, tn), lambda i, j: (0, section * nb + j))

    return pl.pallas_call(
        _conformer_in_kernel,
        grid=(t // tm, n // tn),
        in_specs=[pl.BlockSpec((tm, d), lambda i, j: (i, 0)),
                  pl.BlockSpec((1, d), lambda i, j: (0, 0)),
                  w_spec(0), w_spec(1), w_spec(2)],
        out_specs=[pl.BlockSpec((tn // LANES, tm, LANES), lambda i, j: (j, i, 0)),
                   pl.BlockSpec((tm, tn), lambda i, j: (i, j))],
        out_shape=[jax.ShapeDtypeStruct((n // LANES, t, LANES), BF16),
                   jax.ShapeDtypeStruct((t, n), BF16)],
        scratch_shapes=[pltpu.VMEM((tm, d), BF16)],
        compiler_params=_cparams(("parallel", "arbitrary")),
        name="conformer_in_proj",
    )(h, norm_w.reshape(1, d), w, w, w)


def _proj_residual_kernel(*refs, n_x):
    h_ref = refs[0]
    x_refs = refs[1:1 + n_x]
    w_refs = refs[1 + n_x:1 + 2 * n_x]
    o_ref = refs[1 + 2 * n_x]
    acc = h_ref[...]
    for x_ref, w_ref in zip(x_refs, w_refs):
        x = _load_strips(x_ref) if len(x_ref.shape) == 3 else x_ref[...]
        acc = acc + jnp.dot(x, w_ref[...], preferred_element_type=F32)
    o_ref[...] = acc


def _proj_residual(h, xs, w, *, tm=PROJ_ROWS, tn=OUT_PROJ_COLS):
    t, n = h.shape
    tm = min(tm, t)
    tn = min(tn, n)
    assert t % tm == 0 and n % tn == 0
    k = w.shape[0] // len(xs)
    in_specs = [pl.BlockSpec((tm, tn), lambda i, j: (i, j))]
    for x in xs:
        if x.ndim == 3:
            assert x.shape[0] * LANES == k
            in_specs.append(pl.BlockSpec((x.shape[0], tm, LANES), lambda i, j: (0, i, 0)))
        else:
            assert x.shape[1] == k
            in_specs.append(pl.BlockSpec((tm, k), lambda i, j: (i, 0)))
    in_specs += [pl.BlockSpec((k, tn), lambda i, j, r=r: (r, j)) for r in range(len(xs))]
    return pl.pallas_call(
        functools.partial(_proj_residual_kernel, n_x=len(xs)),
        grid=(t // tm, n // tn),
        in_specs=in_specs,
        out_specs=pl.BlockSpec((tm, tn), lambda i, j: (i, j)),
        out_shape=jax.ShapeDtypeStruct((t, n), F32),
        compiler_params=_cparams(("parallel", "arbitrary")),
        name="proj_residual",
    )(h, *xs, *([w] * len(xs)))


def _ple_kernel(h_ref, nw_ref, wg_ref, p_ref, wp_ref, *rest, final):
    if final:
        fw_ref, o_ref = rest
    else:
        (o_ref,) = rest
    rc = h_ref.shape[0] // PLE_CHUNKS
    for c in range(PLE_CHUNKS):
        rows = slice(c * rc, (c + 1) * rc)
        hc = h_ref[rows, :]
        xn = _rms_rows(hc, nw_ref[...]).astype(BF16)
        gate = _sigmoid(jnp.dot(xn, wg_ref[...], preferred_element_type=F32))
        proj = jnp.dot(p_ref[rows, :].astype(BF16), wp_ref[...], preferred_element_type=F32)
        out = hc + gate * proj
        if final:
            out = _rms_rows(out, fw_ref[...])
        o_ref[rows, :] = out


def _ple(h, norm_w, w_gate, p_all, layer, w_proj, final_w=None, *, tm=PLE_ROWS):
    t, d = h.shape
    e = p_all.shape[2]
    tm = min(tm, t)
    assert t % tm == 0
    final = final_w is not None
    vec = pl.BlockSpec((1, d), lambda i: (0, 0))
    in_specs = [pl.BlockSpec((tm, d), lambda i: (i, 0)), vec,
                pl.BlockSpec((d, d), lambda i: (0, 0)),
                pl.BlockSpec((None, tm, e), lambda i: (layer, i, 0)),
                pl.BlockSpec((e, d), lambda i: (0, 0))]
    args = [h, norm_w.reshape(1, d), w_gate, p_all, w_proj]
    if final:
        in_specs.append(vec)
        args.append(final_w.reshape(1, d))
    return pl.pallas_call(
        functools.partial(_ple_kernel, final=final),
        grid=(t // tm,),
        in_specs=in_specs,
        out_specs=pl.BlockSpec((tm, d), lambda i: (i, 0)),
        out_shape=jax.ShapeDtypeStruct((t, d), F32),
        compiler_params=_cparams(("parallel",)),
        name="per_layer_embedding_final" if final else "per_layer_embedding",
    )(*args)


def _cast_kernel(w_ref, o_ref):
    o_ref[...] = w_ref[0].astype(o_ref.dtype)


def _cast_bf16(w_stack, layer, col0=0, ncols=None, *, tr=CAST_ROWS, tc=CAST_COLS):
    _, r, c = w_stack.shape
    ncols = c - col0 if ncols is None else ncols
    tr = min(tr, r)
    tc = math.gcd(tc, ncols)
    assert r % tr == 0 and col0 % tc == 0 and tc % LANES == 0
    cb = col0 // tc
    return pl.pallas_call(
        _cast_kernel,
        grid=(r // tr, ncols // tc),
        in_specs=[pl.BlockSpec((1, tr, tc), lambda i, j: (layer, i, cb + j))],
        out_specs=pl.BlockSpec((tr, tc), lambda i, j: (i, j)),
        out_shape=jax.ShapeDtypeStruct((r, ncols), BF16),
        compiler_params=_cparams(("parallel", "parallel")),
        name="cast_bf16",
    )(w_stack)


def _cast_shift_kernel(a_ref, b_ref, o_ref, *, shift, first_scale):
    a = a_ref[0]
    x = jnp.concatenate([a[:, shift:], b_ref[0][:, :shift]], axis=-1)
    x = jnp.where(pl.program_id(1) == 0, x * first_scale, x)
    o_ref[...] = x.astype(o_ref.dtype)


def _cast_bf16_unaligned(w_stack, layer, col0, ncols, first_scale, *, tr=CAST_ROWS // 2, tc=CAST_COLS):
    _, r, c = w_stack.shape
    shift = col0 % LANES
    a0 = col0 - shift
    tr = min(tr, r)
    assert r % tr == 0 and ncols % tc == 0 and a0 % tc == 0 and 0 < shift < LANES
    assert a0 + ncols + shift <= c + LANES - 1
    return pl.pallas_call(
        functools.partial(_cast_shift_kernel, shift=shift, first_scale=first_scale),
        grid=(r // tr, ncols // tc),
        in_specs=[pl.BlockSpec((1, tr, tc), lambda i, j: (layer, i, a0 // tc + j)),
                  pl.BlockSpec((1, tr, LANES), lambda i, j: (layer, i, (a0 + tc * (j + 1)) // LANES))],
        out_specs=pl.BlockSpec((tr, tc), lambda i, j: (i, j)),
        out_shape=jax.ShapeDtypeStruct((r, ncols), BF16),
        compiler_params=_cparams(("parallel", "parallel")),
        name="cast_bf16_unaligned",
    )(w_stack, w_stack)


def _split_dot(v, e):
    hi = v.astype(BF16)
    lo = (v - hi.astype(F32)).astype(BF16)
    return (jnp.dot(hi, e, preferred_element_type=F32)
            + jnp.dot(lo, e, preferred_element_type=F32))


def _ssd_kernel(*refs, rev, final, n_chunks, d, heads, groups, conv_k):
    if final:
        (z_ref, xbc_ref, dt_ref, yother_ref,
         dtb_ref, alog_ref, dskip_ref, gw_ref, o_ref, state_ref) = refs
    else:
        (x_ref, bc_ref, xp_ref, bcp_ref, xn_ref, bcn_ref, dt_ref,
         cw_ref, cb_ref, dtb_ref, alog_ref, o_ref, xbc_out_ref, xpad_ref, state_ref) = refs

    q = SSM_CHUNK
    n = SSM_STATE
    hpg = heads // groups
    gw = d // groups
    p_dim = d // heads
    c = pl.program_id(1)
    ci = (n_chunks - 1 - c) if rev else c

    @pl.when(c == 0)
    def _():
        state_ref[...] = jnp.zeros_like(state_ref)

    if final:
        def conv_silu(lo, width):
            return xbc_ref[0, :, lo:lo + width]
    else:
        hl = SSM_HALO
        no_halo = jnp.zeros((hl, d), BF16)
        xpad_ref[0:hl, 0:d] = jnp.where(ci > 0, xp_ref[0], no_halo)
        xpad_ref[0:hl, d:2 * d] = jnp.where(ci > 0, bcp_ref[0], no_halo)
        xpad_ref[hl:hl + q, 0:d] = x_ref[0]
        xpad_ref[hl:hl + q, d:2 * d] = bc_ref[0]
        xpad_ref[hl + q:, 0:d] = jnp.where(ci < n_chunks - 1, xn_ref[0], no_halo)
        xpad_ref[hl + q:, d:2 * d] = jnp.where(ci < n_chunks - 1, bcn_ref[0], no_halo)

        half = conv_k // 2
        taps = [k for k in range(conv_k) if k != half]
        sel_row = lax.broadcasted_iota(jnp.int32, (q, q + 2 * hl), 0)
        sel_col = lax.broadcasted_iota(jnp.int32, (q, q + 2 * hl), 1)
        sel = jnp.concatenate([jnp.where(sel_col == sel_row + (hl + k - half), 1.0, 0.0).astype(BF16)
                               for k in taps], axis=0)
        for lo in range(0, 2 * d, SSM_CONV_COLS):
            csl = slice(lo, lo + SSM_CONV_COLS)
            xc = xpad_ref[:, csl]
            shifted = jnp.dot(sel, xc, preferred_element_type=F32)
            acc = cb_ref[:, csl] + cw_ref[half:half + 1, csl] * xc[hl:hl + q].astype(F32)
            for kk, k in enumerate(taps):
                acc = acc + cw_ref[k:k + 1, csl] * shifted[kk * q:(kk + 1) * q]
            xbc_out_ref[0, :, csl] = _silu(acc).astype(BF16)

        def conv_silu(lo, width):
            return xbc_out_ref[0, :, lo:lo + width]

    dsel = heads if rev else 0
    dt_raw = dt_ref[0][:, dsel:dsel + heads] + dtb_ref[...]
    dt = jnp.maximum(dt_raw, 0.0) + jnp.log1p(jnp.exp(-jnp.abs(dt_raw)))
    a = -jnp.exp(alog_ref[...])
    da = dt * a

    row = lax.broadcasted_iota(jnp.int32, (q, q), 0)
    col = lax.broadcasted_iota(jnp.int32, (q, q), 1)
    causal = (col >= row) if rev else (col <= row)
    tri = causal.astype(F32)
    cs_col = jnp.dot(tri, da, preferred_element_type=F32, precision=lax.Precision.HIGHEST)
    eye_h = (lax.broadcasted_iota(jnp.int32, (heads, heads), 0)
             == lax.broadcasted_iota(jnp.int32, (heads, heads), 1)).astype(F32)
    nt = (((1,), (1,)), ((), ()))
    cs_row = lax.dot_general(eye_h, cs_col, nt, preferred_element_type=F32,
                             precision=lax.Precision.HIGHEST)
    dt_row = lax.dot_general(eye_h, dt, nt, preferred_element_type=F32,
                             precision=lax.Precision.HIGHEST)
    last = 0 if rev else q - 1
    cs_last = cs_col[last:last + 1, :]

    e_lane = lax.broadcasted_iota(jnp.int32, (heads, d), 1)
    e_lo = lax.broadcasted_iota(jnp.int32, (heads, d), 0) * p_dim
    expand = jnp.where((e_lane >= e_lo) & (e_lane < e_lo + p_dim), 1.0, 0.0).astype(BF16)
    ecs_exp = _split_dot(jnp.exp(cs_col), expand)
    dec_exp = _split_dot(jnp.exp(cs_last - cs_col) * dt, expand)

    lane = lax.broadcasted_iota(jnp.int32, (1, gw), 1)

    for g in range(groups):
        xg_b = conv_silu(g * gw, gw)
        bg = conv_silu(d + g * n, n)
        cg = conv_silu(d + groups * n + g * n, n)
        gsl = slice(g * gw, (g + 1) * gw)
        xg = xg_b.astype(F32)

        scores = lax.dot_general(cg, bg, nt, preferred_element_type=F32)
        m_parts = []
        r_parts = []
        for hh in range(hpg):
            h = g * hpg + hh
            diff = cs_col[:, h:h + 1] - cs_row[h:h + 1, :]
            decay = jnp.exp(jnp.where(causal, diff, NEG_BIG))
            m_parts.append((scores * decay * dt_row[h:h + 1, :]).astype(BF16))
            in_head = (lane >= hh * p_dim) & (lane < (hh + 1) * p_dim)
            r_parts.append(jnp.where(in_head, xg_b, jnp.zeros_like(xg_b)))
        m_cat = jnp.concatenate(m_parts, axis=1)
        r_cat = jnp.concatenate(r_parts, axis=0)
        y = jnp.dot(m_cat, r_cat, preferred_element_type=F32)

        st = state_ref[g]
        y = y + jnp.dot(cg, st.astype(BF16), preferred_element_type=F32) * ecs_exp[:, gsl]
        contrib = lax.dot_general(bg, (xg * dec_exp[:, gsl]).astype(BF16),
                                  (((0,), (0,)), ((), ())), preferred_element_type=F32)
        state_ref[g] = st * ecs_exp[last:last + 1, gsl] + contrib

        if final:
            y = y + yother_ref[0][:, gsl] + dskip_ref[:, gsl] * xg
            y = y * _silu(z_ref[0][:, gsl].astype(F32))
            ms = jnp.mean(y * y, axis=-1, keepdims=True)
            y = y * lax.rsqrt(ms + EPS) * gw_ref[:, gsl]
        o_ref[0, :, gsl] = y.astype(o_ref.dtype)


def _ssd_specs(s, d, rev):
    q = SSM_CHUNK
    n_chunks = s // q
    hb = q // SSM_HALO
    n_hblk = s // SSM_HALO

    def cidx(c):
        return (n_chunks - 1 - c) if rev else c

    def main(colblk, width=d):
        return pl.BlockSpec((1, q, width), lambda bi, c: (bi, cidx(c), colblk))

    def prev(colblk):
        return pl.BlockSpec((1, SSM_HALO, d),
                            lambda bi, c: (bi, jnp.maximum(cidx(c) * hb - 1, 0), colblk))

    def nxt(colblk):
        return pl.BlockSpec((1, SSM_HALO, d),
                            lambda bi, c: (bi, jnp.minimum(cidx(c) * hb + hb, n_hblk - 1), colblk))

    def whole(arr):
        return pl.BlockSpec(arr.shape, lambda bi, c: (0,) * arr.ndim)

    return n_chunks, main, prev, nxt, whole


def _ssd_first(u, dt_raw, conv_w, conv_b, dt_bias, a_log, *, heads, groups, d):
    b, s, _ = u.shape
    n_chunks, main, prev, nxt, whole = _ssd_specs(s, d, rev=True)
    params = (conv_w, conv_b.reshape(1, -1), dt_bias.reshape(1, -1), a_log.reshape(1, -1))
    return pl.pallas_call(
        functools.partial(_ssd_kernel, rev=True, final=False, n_chunks=n_chunks, d=d,
                          heads=heads, groups=groups, conv_k=conv_w.shape[0]),
        grid=(b, n_chunks),
        in_specs=[main(1), main(2), prev(1), prev(2), nxt(1), nxt(2), main(0, dt_raw.shape[-1])]
                 + [whole(a) for a in params],
        out_specs=[main(0), main(0, 2 * d)],
        out_shape=[jax.ShapeDtypeStruct((b, s, d), F32),
                   jax.ShapeDtypeStruct((b, s, 2 * d), BF16)],
        scratch_shapes=[pltpu.VMEM((SSM_CHUNK + 2 * SSM_HALO, 2 * d), BF16),
                        pltpu.VMEM((groups, SSM_STATE, d // groups), F32)],
        compiler_params=_cparams(("parallel", "arbitrary")),
        name="ssd_first",
    )(u, u, u, u, u, u, dt_raw, *params)


def _ssd_final(u, xbc, dt_raw, y_other, dt_bias, a_log, d_skip_exp, gnorm_w, *, heads, groups, d):
    b, s, _ = u.shape
    n_chunks, main, _, _, whole = _ssd_specs(s, d, rev=False)
    params = (dt_bias.reshape(1, -1), a_log.reshape(1, -1), d_skip_exp.reshape(1, -1),
              gnorm_w.reshape(1, -1))
    return pl.pallas_call(
        functools.partial(_ssd_kernel, rev=False, final=True, n_chunks=n_chunks, d=d,
                          heads=heads, groups=groups, conv_k=0),
        grid=(b, n_chunks),
        in_specs=[main(0), main(0, 2 * d), main(0, dt_raw.shape[-1]), main(0)]
                 + [whole(a) for a in params],
        out_specs=main(0),
        out_shape=jax.ShapeDtypeStruct((b, s, d), BF16),
        scratch_shapes=[pltpu.VMEM((groups, SSM_STATE, d // groups), F32)],
        compiler_params=_cparams(("parallel", "arbitrary")),
        name="ssd_final",
    )(u, xbc, dt_raw, y_other, *params)


def _na_kernel(q_ref, k_ref, v_ref, g_ref, tbl_ref, o_ref, bias_ref, *, rows, kr, ro_base):
    w = GRID_W
    for dlt in range(kr):
        for i in range(0, kr, 2):
            pair = [tbl_ref[0, ro_base + dlt + i + j] for j in range(2)]
            bias_ref[dlt, :, i * w:(i + 2) * w] = jnp.concatenate(pair, axis=-1)

    def body(it, carry):
        rr = [it * NA_ROW_UNROLL + u for u in range(NA_ROW_UNROLL)]
        rs = [jnp.clip(r - kr // 2, 0, rows - kr) for r in rr]
        q0 = [pl.multiple_of(r * w, w) for r in rr]
        k0 = [pl.multiple_of(x * w, w) for x in rs]
        s = [lax.dot_general(q_ref[0, pl.ds(q0[u], w), :], k_ref[0, pl.ds(k0[u], kr * w), :],
                             (((1,), (1,)), ((), ())), preferred_element_type=F32)
             for u in range(NA_ROW_UNROLL)]
        s = [s[u] + bias_ref[rs[u] - rr[u] + kr - 1] for u in range(NA_ROW_UNROLL)]
        e = [jnp.exp(x - jnp.max(x, axis=-1, keepdims=True)) for x in s]
        l = [jnp.sum(x, axis=-1, keepdims=True) for x in e]
        o = [jnp.dot(e[u].astype(BF16), v_ref[0, pl.ds(k0[u], kr * w), :],
                     preferred_element_type=F32) for u in range(NA_ROW_UNROLL)]
        for u in range(NA_ROW_UNROLL):
            gate = g_ref[0, pl.ds(q0[u], w), :].astype(F32)
            o_ref[0, pl.ds(q0[u], w), :] = (o[u] / l[u] * gate).astype(o_ref.dtype)
        return carry

    lax.fori_loop(0, rows // NA_ROW_UNROLL, body, 0)


def _na_bias_table(rpb):
    win_cols = (rpb.shape[2] + 1) // 2
    cols = jnp.arange(GRID_W)
    col_start = jnp.clip(cols - win_cols // 2, 0, GRID_W - win_cols)
    col_valid = (cols[None, :] >= col_start[:, None]) & (cols[None, :] < col_start[:, None] + win_cols)
    col_off = jnp.clip(cols[None, :] - cols[:, None], -(win_cols - 1), win_cols - 1) + win_cols - 1
    onehot = (col_off[None] == jnp.arange(rpb.shape[2])[:, None, None]).astype(F32)
    tbl = jnp.einsum("hrc,cqk->hrqk", rpb.astype(F32), onehot, precision=lax.Precision.HIGHEST)
    return jnp.where(col_valid[None, None], tbl, NEG_BIG)


def _na_attention(u, rpb, *, b):
    n_sec, t, hd = u.shape
    heads = n_sec // 4
    s = t // b
    rows = s // GRID_W
    n_ro = rpb.shape[1]
    win_rows = (n_ro + 1) // 2
    kr = min(win_rows, rows)
    assert kr % 2 == 0 and 2 * GRID_W == LANES and rows % NA_ROW_UNROLL == 0
    tbl = _na_bias_table(rpb)

    def sec(i):
        return pl.BlockSpec((1, s, hd), lambda bi, h: (i * heads + h, bi, 0))

    return pl.pallas_call(
        functools.partial(_na_kernel, rows=rows, kr=kr, ro_base=win_rows - kr),
        grid=(b, heads),
        in_specs=[sec(0), sec(1), sec(2), sec(3),
                  pl.BlockSpec((1, n_ro, GRID_W, GRID_W), lambda bi, h: (h, 0, 0, 0))],
        out_specs=pl.BlockSpec((1, s, hd), lambda bi, h: (h, bi, 0)),
        out_shape=jax.ShapeDtypeStruct((heads, t, hd), BF16),
        scratch_shapes=[pltpu.VMEM((kr, GRID_W, kr * GRID_W), F32)],
        compiler_params=_cparams(("parallel", "arbitrary")),
        name="neighbourhood_attention",
    )(u, u, u, u, tbl)


def _conformer_kernel(v_ref, vp_ref, vn_ref, g_ref, w_ref, b_ref, lnw_ref, lnb_ref,
                      o_ref, xs_ref, conv_ref, *, n_blocks, width, rows):
    ci = pl.program_id(1)
    hl = CONF_HALO
    n_strips = v_ref.shape[0]
    seg = rows // SUBLANES
    p_in = seg + 2 * hl + SUBLANES
    p_out = seg + SUBLANES
    half = width // 2
    has_prev = (ci > 0).astype(F32)
    has_next = (ci < n_blocks - 1).astype(F32)

    for j in range(SUBLANES):
        base = p_in * j
        lo = seg * j - hl
        hi = seg * j + seg + hl
        if lo < 0:
            xs_ref[:, base:base + hl, :] = vp_ref[...].astype(F32) * has_prev
            xs_ref[:, base + hl:base + hl + hi, :] = v_ref[:, 0:hi, :].astype(F32)
        elif hi > rows:
            xs_ref[:, base:base + rows - lo, :] = v_ref[:, lo:rows, :].astype(F32)
            xs_ref[:, base + rows - lo:base + seg + 2 * hl, :] = vn_ref[...].astype(F32) * has_next
        else:
            xs_ref[:, base:base + seg + 2 * hl, :] = v_ref[:, lo:hi, :].astype(F32)

    off = hl - half

    def strip_body(s, carry):
        wv = [jnp.broadcast_to(w_ref[k, s], (SUBLANES, LANES)) for k in range(width)]
        bv = jnp.broadcast_to(b_ref[s], (SUBLANES, LANES))
        for i0 in range(0, seg, CONF_ACC):
            acc = [bv] * CONF_ACC
            for m in range(i0 + off, i0 + CONF_ACC - 1 + off + width):
                xv = xs_ref[s, pl.ds(m, SUBLANES, stride=p_in), :]
                for i in range(i0, i0 + CONF_ACC):
                    k = m - i - off
                    if 0 <= k < width:
                        acc[i - i0] = acc[i - i0] + wv[k] * xv
            for i in range(i0, i0 + CONF_ACC):
                conv_ref[s, pl.ds(i, SUBLANES, stride=p_out), :] = acc[i - i0]
        return carry

    lax.fori_loop(0, n_strips, strip_body, 0)

    inv_dc = 1.0 / (n_strips * LANES)

    def norm_body(j, carry):
        for rb in range(seg // BF16_ROWS):
            r_in = pl.multiple_of(j * p_out + rb * BF16_ROWS, SUBLANES)
            t0 = pl.multiple_of(j * seg + rb * BF16_ROWS, BF16_ROWS)
            x = conv_ref[:, pl.ds(r_in, BF16_ROWS), :]
            mu = jnp.sum(jnp.sum(x, axis=0), axis=-1, keepdims=True) * inv_dc
            xc = x - mu[None]
            var = jnp.sum(jnp.sum(xc * xc, axis=0), axis=-1, keepdims=True) * inv_dc
            y = _silu(xc * lax.rsqrt(var + EPS)[None] * lnw_ref[...] + lnb_ref[...])
            y = jnp.concatenate([y[s] for s in range(n_strips)], axis=-1).astype(BF16)
            o_ref[0, pl.ds(t0, BF16_ROWS), :] = y * g_ref[0, pl.ds(t0, BF16_ROWS), :]
        return carry

    lax.fori_loop(0, SUBLANES, norm_body, 0)


def _conformer_mix(v, g, dw_w, dw_b, ln_w, ln_b, *, b, rows=CONF_ROWS):
    n_strips, t, _ = v.shape
    dc = n_strips * LANES
    s = t // b
    rows = min(rows, s)
    n_blocks = s // rows
    hb = rows // CONF_HALO
    n_hblk = s // CONF_HALO
    width = dw_w.shape[0]
    seg = rows // SUBLANES
    assert width // 2 <= CONF_HALO <= seg and seg % BF16_ROWS == 0 and seg % CONF_ACC == 0

    def whole(arr):
        return pl.BlockSpec(arr.shape, lambda bi, c: (0,) * arr.ndim)

    params = (dw_w.reshape(width, n_strips, 1, LANES), dw_b.reshape(n_strips, 1, LANES),
              ln_w.reshape(n_strips, 1, LANES), ln_b.reshape(n_strips, 1, LANES))
    return pl.pallas_call(
        functools.partial(_conformer_kernel, n_blocks=n_blocks, width=width, rows=rows),
        grid=(b, n_blocks),
        in_specs=[pl.BlockSpec((n_strips, rows, LANES), lambda bi, c: (0, bi * n_blocks + c, 0)),
                  pl.BlockSpec((n_strips, CONF_HALO, LANES),
                               lambda bi, c: (0, bi * n_hblk + jnp.maximum(c * hb - 1, 0), 0)),
                  pl.BlockSpec((n_strips, CONF_HALO, LANES),
                               lambda bi, c: (0, bi * n_hblk + jnp.minimum(c * hb + hb, n_hblk - 1), 0)),
                  pl.BlockSpec((1, rows, dc), lambda bi, c: (bi, c, 0))]
                 + [whole(a) for a in params],
        out_specs=pl.BlockSpec((1, rows, dc), lambda bi, c: (bi, c, 0)),
        out_shape=jax.ShapeDtypeStruct((b, s, dc), BF16),
        scratch_shapes=[pltpu.VMEM((n_strips, SUBLANES * (seg + 2 * CONF_HALO + SUBLANES), LANES), F32),
                        pltpu.VMEM((n_strips, SUBLANES * (seg + SUBLANES), LANES), F32)],
        compiler_params=_cparams(("parallel", "arbitrary")),
        name="conformer_conv_ln",
    )(v, v, v, g.reshape(b, s, dc), *params)


def _even_layer(h, b, s, e, norm_w, w_in_all, conv_w, conv_b, dt_bias_f, dt_bias_b, a_log_f, a_log_b,
                d_skip, gnorm_w, rpb, w_out_all):
    t, d = h.shape
    heads = d_skip.shape[0]
    conv_ch = conv_w.shape[1]
    groups = (conv_ch - d) // (2 * SSM_STATE)
    assert conv_ch == 2 * d, "SSD conv channels are fetched as two d-wide column blocks"
    na_heads = rpb.shape[0]
    n_dt = 2 * heads
    c_dt = d + conv_ch
    c_q = c_dt + n_dt

    w_ssd = _cast_bf16(w_in_all, e, 0, c_dt, tc=d)
    dt_pad = (-n_dt) % LANES
    w_dt = jnp.pad(w_in_all[e, :, c_dt:c_q], ((0, 0), (0, dt_pad)))
    q_scale = (d // na_heads) ** -0.5
    w_na = _cast_bf16_unaligned(w_in_all, e, c_q, 4 * d, q_scale, tc=d)

    u, dt_raw = _norm_matmul(h, norm_w, w_ssd, BF16, w_side=w_dt)
    u = u.reshape(b, s, -1)
    dt_raw = dt_raw.reshape(b, s, -1)
    assert d // na_heads == LANES, "attention heads are exchanged as 128-lane column strips"
    u_na = _norm_matmul(h, norm_w, w_na, BF16, strips=True, silu_from_col=3 * d)

    d_skip_exp = jnp.repeat(d_skip, d // heads)
    common = dict(heads=heads, groups=groups, d=d)
    y_b, xbc = _ssd_first(u, dt_raw, conv_w, conv_b, dt_bias_b, a_log_b, **common)
    y_ssd = _ssd_final(u, xbc, dt_raw, y_b, dt_bias_f, a_log_f, d_skip_exp, gnorm_w, **common)
    y_na = _na_attention(u_na, rpb, b=b)

    return _proj_residual(h, [y_ssd.reshape(t, d), y_na], _cast_bf16(w_out_all, e))


def _odd_layer(h, b, s, e, norm_w, w_in_all, dw_w, dw_b, ln_w, ln_b, w_out_all):
    v, g = _conformer_in(h, norm_w, _cast_bf16(w_in_all, e))
    y = _conformer_mix(v, g, dw_w, dw_b, ln_w, ln_b, b=b)
    return _proj_residual(h, [y.reshape(h.shape[0], -1)], _cast_bf16(w_out_all, e))


def kernel(x, p, ev_norm_w, ev_w_in, ev_conv_w, ev_conv_b, ev_dt_bias_f, ev_dt_bias_b, ev_a_log_f, ev_a_log_b, ev_d_skip, ev_gnorm_w, ev_rpb, ev_w_out, od_norm_w, od_w_in, od_dw_w, od_dw_b, od_ln_w, od_ln_b, od_w_out, ple_norm_w, ple_w_gate, ple_w_proj, final_norm_w):
    b, s, d = x.shape
    depth = p.shape[0]
    h = x.reshape(b * s, d)
    p_all = p.reshape(depth, b * s, -1)
    for i in range(depth):
        e = i // 2
        if i % 2 == 0:
            h = _even_layer(h, b, s, e, ev_norm_w[e], ev_w_in, ev_conv_w[e], ev_conv_b[e],
                            ev_dt_bias_f[e], ev_dt_bias_b[e], ev_a_log_f[e], ev_a_log_b[e],
                            ev_d_skip[e], ev_gnorm_w[e], ev_rpb[e], ev_w_out)
        else:
            h = _odd_layer(h, b, s, e, od_norm_w[e], od_w_in, od_dw_w[e], od_dw_b[e],
                           od_ln_w[e], od_ln_b[e], od_w_out)
        h = _ple(h, ple_norm_w[i], _cast_bf16(ple_w_gate, i), p_all, i,
                 _cast_bf16(ple_w_proj, i), final_norm_w if i == depth - 1 else None)
    return h.reshape(b, s, d)
```

```python
import functools
import math

import jax
import jax.numpy as jnp
from jax import lax
from jax.experimental import pallas as pl
from jax.experimental.pallas import tpu as pltpu

F32 = jnp.float32
BF16 = jnp.bfloat16

EPS = 1e-6
LANES = 128
SUBLANES = 8
GRID_W = 64
SSM_STATE = 128
SSM_CHUNK = 128
SSM_HALO = 16
SSM_CONV_COLS = 256
CONF_HALO = 16
CONF_ACC = 16
NEG_BIG = -1e30
BF16_ROWS = 16
NORM_CHUNKS = 4
PLE_CHUNKS = 2
NA_ROW_UNROLL = 8

V7X_VMEM_LIMIT = 56 * 1024 * 1024

PROJ_ROWS = 1024
NORM_PROJ_COLS = 2048
CONF_PROJ_COLS = 512
OUT_PROJ_COLS = 1024
PLE_ROWS = 512
CAST_ROWS = 512
CAST_COLS = 2048
CONF_ROWS = 256


def _cparams(sem):
    return pltpu.CompilerParams(dimension_semantics=sem, vmem_limit_bytes=V7X_VMEM_LIMIT)


def _sigmoid(x):
    return 1.0 / (1.0 + jnp.exp(-x))


def _silu(x):
    return x * _sigmoid(x)


def _rms_rows(x, w):
    ms = jnp.mean(x * x, axis=-1, keepdims=True)
    return x * lax.rsqrt(ms + EPS) * w


def _store_strips(o_ref, rows, val):
    for c in range(o_ref.shape[0]):
        o_ref[c, rows, :] = val[:, c * LANES:(c + 1) * LANES].astype(o_ref.dtype)


def _load_strips(x_ref):
    return jnp.concatenate([x_ref[c] for c in range(x_ref.shape[0])], axis=-1)


def _for_normed_rows(h_ref, nw_ref, xn_ref, emit):
    tm = h_ref.shape[0]
    rc = tm // NORM_CHUNKS

    @pl.when(pl.program_id(1) == 0)
    def _():
        for c in range(NORM_CHUNKS):
            rows = slice(c * rc, (c + 1) * rc)
            xn = _rms_rows(h_ref[rows, :], nw_ref[...]).astype(BF16)
            xn_ref[rows, :] = xn
            emit(rows, xn, True)

    @pl.when(pl.program_id(1) > 0)
    def _():
        emit(slice(0, tm), xn_ref[...], False)


def _norm_matmul_kernel(h_ref, nw_ref, w_ref, *rest, strips, silu_from, side):
    if side:
        w2_ref, o_ref, o2_ref, xn_ref = rest
    else:
        o_ref, xn_ref = rest

    def plain(rows, xn):
        acc = jnp.dot(xn, w_ref[...], preferred_element_type=F32)
        if strips:
            _store_strips(o_ref, rows, acc)
        else:
            o_ref[rows, :] = acc.astype(o_ref.dtype)

    def activated(rows, xn):
        rc = (rows.stop - rows.start) // NORM_CHUNKS
        for c in range(NORM_CHUNKS):
            acc = jnp.dot(xn[c * rc:(c + 1) * rc], w_ref[...], preferred_element_type=F32)
            sub = slice(rows.start + c * rc, rows.start + (c + 1) * rc)
            _store_strips(o_ref, sub, _silu(acc))

    def emit(rows, xn, first_step):
        if side and first_step:
            o2_ref[rows, :] = jnp.dot(xn, w2_ref[...].astype(BF16), preferred_element_type=F32)
        if silu_from is None or first_step:
            plain(rows, xn)
        else:
            pl.when(pl.program_id(1) < silu_from)(lambda: plain(rows, xn))
            pl.when(pl.program_id(1) >= silu_from)(lambda: activated(rows, xn))

    _for_normed_rows(h_ref, nw_ref, xn_ref, emit)


def _norm_matmul(h, norm_w, w, out_dtype, *, strips=False, silu_from_col=None, w_side=None,
                 tm=PROJ_ROWS, tn=NORM_PROJ_COLS):
    t, d = h.shape
    n = w.shape[1]
    tm = min(tm, t)
    tn = math.gcd(tn, n)
    silu_from = None
    if silu_from_col is not None:
        assert strips and silu_from_col > 0
        tn = math.gcd(tn, silu_from_col)
        silu_from = silu_from_col // tn
    assert t % tm == 0 and tn % LANES == 0
    if strips:
        out_specs = [pl.BlockSpec((tn // LANES, tm, LANES), lambda i, j: (j, i, 0))]
        out_shape = [jax.ShapeDtypeStruct((n // LANES, t, LANES), out_dtype)]
    else:
        out_specs = [pl.BlockSpec((tm, tn), lambda i, j: (i, j))]
        out_shape = [jax.ShapeDtypeStruct((t, n), out_dtype)]
    in_specs = [pl.BlockSpec((tm, d), lambda i, j: (i, 0)),
                pl.BlockSpec((1, d), lambda i, j: (0, 0)),
                pl.BlockSpec((d, tn), lambda i, j: (0, j))]
    args = [h, norm_w.reshape(1, d), w]
    side = w_side is not None
    if side:
        ns = w_side.shape[1]
        in_specs.append(pl.BlockSpec((d, ns), lambda i, j: (0, 0)))
        args.append(w_side)
        out_specs.append(pl.BlockSpec((tm, ns), lambda i, j: (i, 0)))
        out_shape.append(jax.ShapeDtypeStruct((t, ns), F32))
    outs = pl.pallas_call(
        functools.partial(_norm_matmul_kernel, strips=strips, silu_from=silu_from, side=side),
        grid=(t // tm, n // tn),
        in_specs=in_specs,
        out_specs=out_specs,
        out_shape=out_shape,
        scratch_shapes=[pltpu.VMEM((tm, d), BF16)],
        compiler_params=_cparams(("parallel", "arbitrary")),
        name="norm_matmul_strips" if strips else "norm_matmul",
    )(*args)
    return outs if side else outs[0]


def _conformer_in_kernel(h_ref, nw_ref, wa_ref, wb_ref, wg_ref, v_ref, g_ref, xn_ref):
    def emit(rows, xn, first_step):
        a = jnp.dot(xn, wa_ref[...], preferred_element_type=F32)
        a = a * _sigmoid(jnp.dot(xn, wb_ref[...], preferred_element_type=F32))
        _store_strips(v_ref, rows, a)
        g = _silu(jnp.dot(xn, wg_ref[...], preferred_element_type=F32))
        g_ref[rows, :] = g.astype(g_ref.dtype)

    _for_normed_rows(h_ref, nw_ref, xn_ref, emit)


def _conformer_in(h, norm_w, w, *, tm=PROJ_ROWS, tn=CONF_PROJ_COLS):
    t, d = h.shape
    n = w.shape[1] // 3
    tm = min(tm, t)
    tn = min(tn, n)
    assert t % tm == 0 and n % tn == 0 and tn % LANES == 0
    nb = n // tn

    def w_spec(section):
        return pl.BlockSpec((d, tn), lambda i, j: (0, section * nb + j))

    return pl.pallas_call(
        _conformer_in_kernel,
        grid=(t // tm, n // tn),
        in_specs=[pl.BlockSpec((tm, d), lambda i, j: (i, 0)),
                  pl.BlockSpec((1, d), lambda i, j: (0, 0)),
                  w_spec(0), w_spec(1), w_spec(2)],
        out_specs=[pl.BlockSpec((tn // LANES, tm, LANES), lambda i, j: (j, i, 0)),
                   pl.BlockSpec((tm, tn), lambda i, j: (i, j))],
        out_shape=[jax.ShapeDtypeStruct((n // LANES, t, LANES), BF16),
                   jax.ShapeDtypeStruct((t, n), BF16)],
        scratch_shapes=[pltpu.VMEM((tm, d), BF16)],
        compiler_params=_cparams(("parallel", "arbitrary")),
        name="conformer_in_proj",
    )(h, norm_w.reshape(1, d), w, w, w)


def _proj_residual_kernel(*refs, n_x):
    h_ref = refs[0]
    x_refs = refs[1:1 + n_x]
    w_refs = refs[1 + n_x:1 + 2 * n_x]
    o_ref = refs[1 + 2 * n_x]
    acc = h_ref[...]
    for x_ref, w_ref in zip(x_refs, w_refs):
        x = _load_strips(x_ref) if len(x_ref.shape) == 3 else x_ref[...]
        acc = acc + jnp.dot(x, w_ref[...], preferred_element_type=F32)
    o_ref[...] = acc


def _proj_residual(h, xs, w, *, tm=PROJ_ROWS, tn=OUT_PROJ_COLS):
    t, n = h.shape
    tm = min(tm, t)
    tn = min(tn, n)
    assert t % tm == 0 and n % tn == 0
    k = w.shape[0] // len(xs)
    in_specs = [pl.BlockSpec((tm, tn), lambda i, j: (i, j))]
    for x in xs:
        if x.ndim == 3:
            assert x.shape[0] * LANES == k
            in_specs.append(pl.BlockSpec((x.shape[0], tm, LANES), lambda i, j: (0, i, 0)))
        else:
            assert x.shape[1] == k
            in_specs.append(pl.BlockSpec((tm, k), lambda i, j: (i, 0)))
    in_specs += [pl.BlockSpec((k, tn), lambda i, j, r=r: (r, j)) for r in range(len(xs))]
    return pl.pallas_call(
        functools.partial(_proj_residual_kernel, n_x=len(xs)),
        grid=(t // tm, n // tn),
        in_specs=in_specs,
        out_specs=pl.BlockSpec((tm, tn), lambda i, j: (i, j)),
        out_shape=jax.ShapeDtypeStruct((t, n), F32),
        compiler_params=_cparams(("parallel", "arbitrary")),
        name="proj_residual",
    )(h, *xs, *([w] * len(xs)))


def _ple_kernel(h_ref, nw_ref, wg_ref, p_ref, wp_ref, *rest, final):
    if final:
        fw_ref, o_ref = rest
    else:
        (o_ref,) = rest
    rc = h_ref.shape[0] // PLE_CHUNKS
    for c in range(PLE_CHUNKS):
        rows = slice(c * rc, (c + 1) * rc)
        hc = h_ref[rows, :]
        xn = _rms_rows(hc, nw_ref[...]).astype(BF16)
        gate = _sigmoid(jnp.dot(xn, wg_ref[...], preferred_element_type=F32))
        proj = jnp.dot(p_ref[rows, :].astype(BF16), wp_ref[...], preferred_element_type=F32)
        out = hc + gate * proj
        if final:
            out = _rms_rows(out, fw_ref[...])
        o_ref[rows, :] = out


def _ple(h, norm_w, w_gate, p_all, layer, w_proj, final_w=None, *, tm=PLE_ROWS):
    t, d = h.shape
    e = p_all.shape[2]
    tm = min(tm, t)
    assert t % tm == 0
    final = final_w is not None
    vec = pl.BlockSpec((1, d), lambda i: (0, 0))
    in_specs = [pl.BlockSpec((tm, d), lambda i: (i, 0)), vec,
                pl.BlockSpec((d, d), lambda i: (0, 0)),
                pl.BlockSpec((None, tm, e), lambda i: (layer, i, 0)),
                pl.BlockSpec((e, d), lambda i: (0, 0))]
    args = [h, norm_w.reshape(1, d), w_gate, p_all, w_proj]
    if final:
        in_specs.append(vec)
        args.append(final_w.reshape(1, d))
    return pl.pallas_call(
        functools.partial(_ple_kernel, final=final),
        grid=(t // tm,),
        in_specs=in_specs,
        out_specs=pl.BlockSpec((tm, d), lambda i: (i, 0)),
        out_shape=jax.ShapeDtypeStruct((t, d), F32),
        compiler_params=_cparams(("parallel",)),
        name="per_layer_embedding_final" if final else "per_layer_embedding",
    )(*args)


def _cast_kernel(w_ref, o_ref):
    o_ref[...] = w_ref[0].astype(o_ref.dtype)


def _cast_bf16(w_stack, layer, col0=0, ncols=None, *, tr=CAST_ROWS, tc=CAST_COLS):
    _, r, c = w_stack.shape
    ncols = c - col0 if ncols is None else ncols
    tr = min(tr, r)
    tc = math.gcd(tc, ncols)
    assert r % tr == 0 and col0 % tc == 0 and tc % LANES == 0
    cb = col0 // tc
    return pl.pallas_call(
        _cast_kernel,
        grid=(r // tr, ncols // tc),
        in_specs=[pl.BlockSpec((1, tr, tc), lambda i, j: (layer, i, cb + j))],
        out_specs=pl.BlockSpec((tr, tc), lambda i, j: (i, j)),
        out_shape=jax.ShapeDtypeStruct((r, ncols), BF16),
        compiler_params=_cparams(("parallel", "parallel")),
        name="cast_bf16",
    )(w_stack)


def _cast_shift_kernel(a_ref, b_ref, o_ref, *, shift, first_scale):
    a = a_ref[0]
    x = jnp.concatenate([a[:, shift:], b_ref[0][:, :shift]], axis=-1)
    x = jnp.where(pl.program_id(1) == 0, x * first_scale, x)
    o_ref[...] = x.astype(o_ref.dtype)


def _cast_bf16_unaligned(w_stack, layer, col0, ncols, first_scale, *, tr=CAST_ROWS // 2, tc=CAST_COLS):
    _, r, c = w_stack.shape
    shift = col0 % LANES
    a0 = col0 - shift
    tr = min(tr, r)
    assert r % tr == 0 and ncols % tc == 0 and a0 % tc == 0 and 0 < shift < LANES
    assert a0 + ncols + shift <= c + LANES - 1
    return pl.pallas_call(
        functools.partial(_cast_shift_kernel, shift=shift, first_scale=first_scale),
        grid=(r // tr, ncols // tc),
        in_specs=[pl.BlockSpec((1, tr, tc), lambda i, j: (layer, i, a0 // tc + j)),
                  pl.BlockSpec((1, tr, LANES), lambda i, j: (layer, i, (a0 + tc * (j + 1)) // LANES))],
        out_specs=pl.BlockSpec((tr, tc), lambda i, j: (i, j)),
        out_shape=jax.ShapeDtypeStruct((r, ncols), BF16),
        compiler_params=_cparams(("parallel", "parallel")),
        name="cast_bf16_unaligned",
    )(w_stack, w_stack)


def _split_dot(v, e):
    hi = v.astype(BF16)
    lo = (v - hi.astype(F32)).astype(BF16)
    return (jnp.dot(hi, e, preferred_element_type=F32)
            + jnp.dot(lo, e, preferred_element_type=F32))


def _ssd_kernel(*refs, rev, final, n_chunks, d, heads, groups, conv_k):
    if final:
        (z_ref, xbc_ref, dt_ref, yother_ref,
         dtb_ref, alog_ref, dskip_ref, gw_ref, o_ref, state_ref) = refs
    else:
        (x_ref, bc_ref, xp_ref, bcp_ref, xn_ref, bcn_ref, dt_ref,
         cw_ref, cb_ref, dtb_ref, alog_ref, o_ref, xbc_out_ref, xpad_ref, state_ref) = refs

    q = SSM_CHUNK
    n = SSM_STATE
    hpg = heads // groups
    gw = d // groups
    p_dim = d // heads
    c = pl.program_id(1)
    ci = (n_chunks - 1 - c) if rev else c

    @pl.when(c == 0)
    def _():
        state_ref[...] = jnp.zeros_like(state_ref)

    if final:
        def conv_silu(lo, width):
            return xbc_ref[0, :, lo:lo + width]
    else:
        hl = SSM_HALO
        no_halo = jnp.zeros((hl, d), BF16)
        xpad_ref[0:hl, 0:d] = jnp.where(ci > 0, xp_ref[0], no_halo)
        xpad_ref[0:hl, d:2 * d] = jnp.where(ci > 0, bcp_ref[0], no_halo)
        xpad_ref[hl:hl + q, 0:d] = x_ref[0]
        xpad_ref[hl:hl + q, d:2 * d] = bc_ref[0]
        xpad_ref[hl + q:, 0:d] = jnp.where(ci < n_chunks - 1, xn_ref[0], no_halo)
        xpad_ref[hl + q:, d:2 * d] = jnp.where(ci < n_chunks - 1, bcn_ref[0], no_halo)

        half = conv_k // 2
        taps = [k for k in range(conv_k) if k != half]
        sel_row = lax.broadcasted_iota(jnp.int32, (q, q + 2 * hl), 0)
        sel_col = lax.broadcasted_iota(jnp.int32, (q, q + 2 * hl), 1)
        sel = jnp.concatenate([jnp.where(sel_col == sel_row + (hl + k - half), 1.0, 0.0).astype(BF16)
                               for k in taps], axis=0)
        for lo in range(0, 2 * d, SSM_CONV_COLS):
            csl = slice(lo, lo + SSM_CONV_COLS)
            xc = xpad_ref[:, csl]
            shifted = jnp.dot(sel, xc, preferred_element_type=F32)
            acc = cb_ref[:, csl] + cw_ref[half:half + 1, csl] * xc[hl:hl + q].astype(F32)
            for kk, k in enumerate(taps):
                acc = acc + cw_ref[k:k + 1, csl] * shifted[kk * q:(kk + 1) * q]
            xbc_out_ref[0, :, csl] = _silu(acc).astype(BF16)

        def conv_silu(lo, width):
            return xbc_out_ref[0, :, lo:lo + width]

    dsel = heads if rev else 0
    dt_raw = dt_ref[0][:, dsel:dsel + heads] + dtb_ref[...]
    dt = jnp.maximum(dt_raw, 0.0) + jnp.log1p(jnp.exp(-jnp.abs(dt_raw)))
    a = -jnp.exp(alog_ref[...])
    da = dt * a

    row = lax.broadcasted_iota(jnp.int32, (q, q), 0)
    col = lax.broadcasted_iota(jnp.int32, (q, q), 1)
    causal = (col >= row) if rev else (col <= row)
    tri = causal.astype(F32)
    cs_col = jnp.dot(tri, da, preferred_element_type=F32, precision=lax.Precision.HIGHEST)
    eye_h = (lax.broadcasted_iota(jnp.int32, (heads, heads), 0)
             == lax.broadcasted_iota(jnp.int32, (heads, heads), 1)).astype(F32)
    nt = (((1,), (1,)), ((), ()))
    cs_row = lax.dot_general(eye_h, cs_col, nt, preferred_element_type=F32,
                             precision=lax.Precision.HIGHEST)
    dt_row = lax.dot_general(eye_h, dt, nt, preferred_element_type=F32,
                             precision=lax.Precision.HIGHEST)
    last = 0 if rev else q - 1
    cs_last = cs_col[last:last + 1, :]

    e_lane = lax.broadcasted_iota(jnp.int32, (heads, d), 1)
    e_lo = lax.broadcasted_iota(jnp.int32, (heads, d), 0) * p_dim
    expand = jnp.where((e_lane >= e_lo) & (e_lane < e_lo + p_dim), 1.0, 0.0).astype(BF16)
    ecs_exp = _split_dot(jnp.exp(cs_col), expand)
    dec_exp = _split_dot(jnp.exp(cs_last - cs_col) * dt, expand)

    lane = lax.broadcasted_iota(jnp.int32, (1, gw), 1)

    for g in range(groups):
        xg_b = conv_silu(g * gw, gw)
        bg = conv_silu(d + g * n, n)
        cg = conv_silu(d + groups * n + g * n, n)
        gsl = slice(g * gw, (g + 1) * gw)
        xg = xg_b.astype(F32)

        scores = lax.dot_general(cg, bg, nt, preferred_element_type=F32)
        m_parts = []
        r_parts = []
        for hh in range(hpg):
            h = g * hpg + hh
            diff = cs_col[:, h:h + 1] - cs_row[h:h + 1, :]
            decay = jnp.exp(jnp.where(causal, diff, NEG_BIG))
            m_parts.append((scores * decay * dt_row[h:h + 1, :]).astype(BF16))
            in_head = (lane >= hh * p_dim) & (lane < (hh + 1) * p_dim)
            r_parts.append(jnp.where(in_head, xg_b, jnp.zeros_like(xg_b)))
        m_cat = jnp.concatenate(m_parts, axis=1)
        r_cat = jnp.concatenate(r_parts, axis=0)
        y = jnp.dot(m_cat, r_cat, preferred_element_type=F32)

        st = state_ref[g]
        y = y + jnp.dot(cg, st.astype(BF16), preferred_element_type=F32) * ecs_exp[:, gsl]
        contrib = lax.dot_general(bg, (xg * dec_exp[:, gsl]).astype(BF16),
                                  (((0,), (0,)), ((), ())), preferred_element_type=F32)
        state_ref[g] = st * ecs_exp[last:last + 1, gsl] + contrib

        if final:
            y = y + yother_ref[0][:, gsl] + dskip_ref[:, gsl] * xg
            y = y * _silu(z_ref[0][:, gsl].astype(F32))
            ms = jnp.mean(y * y, axis=-1, keepdims=True)
            y = y * lax.rsqrt(ms + EPS) * gw_ref[:, gsl]
        o_ref[0, :, gsl] = y.astype(o_ref.dtype)


def _ssd_specs(s, d, rev):
    q = SSM_CHUNK
    n_chunks = s // q
    hb = q // SSM_HALO
    n_hblk = s // SSM_HALO

    def cidx(c):
        return (n_chunks - 1 - c) if rev else c

    def main(colblk, width=d):
        return pl.BlockSpec((1, q, width), lambda bi, c: (bi, cidx(c), colblk))

    def prev(colblk):
        return pl.BlockSpec((1, SSM_HALO, d),
                            lambda bi, c: (bi, jnp.maximum(cidx(c) * hb - 1, 0), colblk))

    def nxt(colblk):
        return pl.BlockSpec((1, SSM_HALO, d),
                            lambda bi, c: (bi, jnp.minimum(cidx(c) * hb + hb, n_hblk - 1), colblk))

    def whole(arr):
        return pl.BlockSpec(arr.shape, lambda bi, c: (0,) * arr.ndim)

    return n_chunks, main, prev, nxt, whole


def _ssd_first(u, dt_raw, conv_w, conv_b, dt_bias, a_log, *, heads, groups, d):
    b, s, _ = u.shape
    n_chunks, main, prev, nxt, whole = _ssd_specs(s, d, rev=True)
    params = (conv_w, conv_b.reshape(1, -1), dt_bias.reshape(1, -1), a_log.reshape(1, -1))
    return pl.pallas_call(
        functools.partial(_ssd_kernel, rev=True, final=False, n_chunks=n_chunks, d=d,
                          heads=heads, groups=groups, conv_k=conv_w.shape[0]),
        grid=(b, n_chunks),
        in_specs=[main(1), main(2), prev(1), prev(2), nxt(1), nxt(2), main(0, dt_raw.shape[-1])]
                 + [whole(a) for a in params],
        out_specs=[main(0), main(0, 2 * d)],
        out_shape=[jax.ShapeDtypeStruct((b, s, d), F32),
                   jax.ShapeDtypeStruct((b, s, 2 * d), BF16)],
        scratch_shapes=[pltpu.VMEM((SSM_CHUNK + 2 * SSM_HALO, 2 * d), BF16),
                        pltpu.VMEM((groups, SSM_STATE, d // groups), F32)],
        compiler_params=_cparams(("parallel", "arbitrary")),
        name="ssd_first",
    )(u, u, u, u, u, u, dt_raw, *params)


def _ssd_final(u, xbc, dt_raw, y_other, dt_bias, a_log, d_skip_exp, gnorm_w, *, heads, groups, d):
    b, s, _ = u.shape
    n_chunks, main, _, _, whole = _ssd_specs(s, d, rev=False)
    params = (dt_bias.reshape(1, -1), a_log.reshape(1, -1), d_skip_exp.reshape(1, -1),
              gnorm_w.reshape(1, -1))
    return pl.pallas_call(
        functools.partial(_ssd_kernel, rev=False, final=True, n_chunks=n_chunks, d=d,
                          heads=heads, groups=groups, conv_k=0),
        grid=(b, n_chunks),
        in_specs=[main(0), main(0, 2 * d), main(0, dt_raw.shape[-1]), main(0)]
                 + [whole(a) for a in params],
        out_specs=main(0),
        out_shape=jax.ShapeDtypeStruct((b, s, d), BF16),
        scratch_shapes=[pltpu.VMEM((groups, SSM_STATE, d // groups), F32)],
        compiler_params=_cparams(("parallel", "arbitrary")),
        name="ssd_final",
    )(u, xbc, dt_raw, y_other, *params)


def _na_kernel(q_ref, k_ref, v_ref, g_ref, tbl_ref, o_ref, bias_ref, *, rows, kr, ro_base):
    w = GRID_W
    for dlt in range(kr):
        for i in range(0, kr, 2):
            pair = [tbl_ref[0, ro_base + dlt + i + j] for j in range(2)]
            bias_ref[dlt, :, i * w:(i + 2) * w] = jnp.concatenate(pair, axis=-1)

    def body(it, carry):
        rr = [it * NA_ROW_UNROLL + u for u in range(NA_ROW_UNROLL)]
        rs = [jnp.clip(r - kr // 2, 0, rows - kr) for r in rr]
        q0 = [pl.multiple_of(r * w, w) for r in rr]
        k0 = [pl.multiple_of(x * w, w) for x in rs]
        s = [lax.dot_general(q_ref[0, pl.ds(q0[u], w), :], k_ref[0, pl.ds(k0[u], kr * w), :],
                             (((1,), (1,)), ((), ())), preferred_element_type=F32)
             for u in range(NA_ROW_UNROLL)]
        s = [s[u] + bias_ref[rs[u] - rr[u] + kr - 1] for u in range(NA_ROW_UNROLL)]
        e = [jnp.exp(x - jnp.max(x, axis=-1, keepdims=True)) for x in s]
        l = [jnp.sum(x, axis=-1, keepdims=True) for x in e]
        o = [jnp.dot(e[u].astype(BF16), v_ref[0, pl.ds(k0[u], kr * w), :],
                     preferred_element_type=F32) for u in range(NA_ROW_UNROLL)]
        for u in range(NA_ROW_UNROLL):
            gate = g_ref[0, pl.ds(q0[u], w), :].astype(F32)
            o_ref[0, pl.ds(q0[u], w), :] = (o[u] / l[u] * gate).astype(o_ref.dtype)
        return carry

    lax.fori_loop(0, rows // NA_ROW_UNROLL, body, 0)


def _na_bias_table(rpb):
    win_cols = (rpb.shape[2] + 1) // 2
    cols = jnp.arange(GRID_W)
    col_start = jnp.clip(cols - win_cols // 2, 0, GRID_W - win_cols)
    col_valid = (cols[None, :] >= col_start[:, None]) & (cols[None, :] < col_start[:, None] + win_cols)
    col_off = jnp.clip(cols[None, :] - cols[:, None], -(win_cols - 1), win_cols - 1) + win_cols - 1
    onehot = (col_off[None] == jnp.arange(rpb.shape[2])[:, None, None]).astype(F32)
    tbl = jnp.einsum("hrc,cqk->hrqk", rpb.astype(F32), onehot, precision=lax.Precision.HIGHEST)
    return jnp.where(col_valid[None, None], tbl, NEG_BIG)


def _na_attention(u, rpb, *, b):
    n_sec, t, hd = u.shape
    heads = n_sec // 4
    s = t // b
    rows = s // GRID_W
    n_ro = rpb.shape[1]
    win_rows = (n_ro + 1) // 2
    kr = min(win_rows, rows)
    assert kr % 2 == 0 and 2 * GRID_W == LANES and rows % NA_ROW_UNROLL == 0
    tbl = _na_bias_table(rpb)

    def sec(i):
        return pl.BlockSpec((1, s, hd), lambda bi, h: (i * heads + h, bi, 0))

    return pl.pallas_call(
        functools.partial(_na_kernel, rows=rows, kr=kr, ro_base=win_rows - kr),
        grid=(b, heads),
        in_specs=[sec(0), sec(1), sec(2), sec(3),
                  pl.BlockSpec((1, n_ro, GRID_W, GRID_W), lambda bi, h: (h, 0, 0, 0))],
        out_specs=pl.BlockSpec((1, s, hd), lambda bi, h: (h, bi, 0)),
        out_shape=jax.ShapeDtypeStruct((heads, t, hd), BF16),
        scratch_shapes=[pltpu.VMEM((kr, GRID_W, kr * GRID_W), F32)],
        compiler_params=_cparams(("parallel", "arbitrary")),
        name="neighbourhood_attention",
    )(u, u, u, u, tbl)


def _conformer_kernel(v_ref, vp_ref, vn_ref, g_ref, w_ref, b_ref, lnw_ref, lnb_ref,
                      o_ref, xs_ref, conv_ref, *, n_blocks, width, rows):
    ci = pl.program_id(1)
    hl = CONF_HALO
    n_strips = v_ref.shape[0]
    seg = rows // SUBLANES
    p_in = seg + 2 * hl + SUBLANES
    p_out = seg + SUBLANES
    half = width // 2
    has_prev = (ci > 0).astype(F32)
    has_next = (ci < n_blocks - 1).astype(F32)

    for j in range(SUBLANES):
        base = p_in * j
        lo = seg * j - hl
        hi = seg * j + seg + hl
        if lo < 0:
            xs_ref[:, base:base + hl, :] = vp_ref[...].astype(F32) * has_prev
            xs_ref[:, base + hl:base + hl + hi, :] = v_ref[:, 0:hi, :].astype(F32)
        elif hi > rows:
            xs_ref[:, base:base + rows - lo, :] = v_ref[:, lo:rows, :].astype(F32)
            xs_ref[:, base + rows - lo:base + seg + 2 * hl, :] = vn_ref[...].astype(F32) * has_next
        else:
            xs_ref[:, base:base + seg + 2 * hl, :] = v_ref[:, lo:hi, :].astype(F32)

    off = hl - half

    def strip_body(s, carry):
        wv = [jnp.broadcast_to(w_ref[k, s], (SUBLANES, LANES)) for k in range(width)]
        bv = jnp.broadcast_to(b_ref[s], (SUBLANES, LANES))
        for i0 in range(0, seg, CONF_ACC):
            acc = [bv] * CONF_ACC
            for m in range(i0 + off, i0 + CONF_ACC - 1 + off + width):
                xv = xs_ref[s, pl.ds(m, SUBLANES, stride=p_in), :]
                for i in range(i0, i0 + CONF_ACC):
                    k = m - i - off
                    if 0 <= k < width:
                        acc[i - i0] = acc[i - i0] + wv[k] * xv
            for i in range(i0, i0 + CONF_ACC):
                conv_ref[s, pl.ds(i, SUBLANES, stride=p_out), :] = acc[i - i0]
        return carry

    lax.fori_loop(0, n_strips, strip_body, 0)

    inv_dc = 1.0 / (n_strips * LANES)

    def norm_body(j, carry):
        for rb in range(seg // BF16_ROWS):
            r_in = pl.multiple_of(j * p_out + rb * BF16_ROWS, SUBLANES)
            t0 = pl.multiple_of(j * seg + rb * BF16_ROWS, BF16_ROWS)
            x = conv_ref[:, pl.ds(r_in, BF16_ROWS), :]
            mu = jnp.sum(jnp.sum(x, axis=0), axis=-1, keepdims=True) * inv_dc
            xc = x - mu[None]
            var = jnp.sum(jnp.sum(xc * xc, axis=0), axis=-1, keepdims=True) * inv_dc
            y = _silu(xc * lax.rsqrt(var + EPS)[None] * lnw_ref[...] + lnb_ref[...])
            y = jnp.concatenate([y[s] for s in range(n_strips)], axis=-1).astype(BF16)
            o_ref[0, pl.ds(t0, BF16_ROWS), :] = y * g_ref[0, pl.ds(t0, BF16_ROWS), :]
        return carry

    lax.fori_loop(0, SUBLANES, norm_body, 0)


def _conformer_mix(v, g, dw_w, dw_b, ln_w, ln_b, *, b, rows=CONF_ROWS):
    n_strips, t, _ = v.shape
    dc = n_strips * LANES
    s = t // b
    rows = min(rows, s)
    n_blocks = s // rows
    hb = rows // CONF_HALO
    n_hblk = s // CONF_HALO
    width = dw_w.shape[0]
    seg = rows // SUBLANES
    assert width // 2 <= CONF_HALO <= seg and seg % BF16_ROWS == 0 and seg % CONF_ACC == 0

    def whole(arr):
        return pl.BlockSpec(arr.shape, lambda bi, c: (0,) * arr.ndim)

    params = (dw_w.reshape(width, n_strips, 1, LANES), dw_b.reshape(n_strips, 1, LANES),
              ln_w.reshape(n_strips, 1, LANES), ln_b.reshape(n_strips, 1, LANES))
    return pl.pallas_call(
        functools.partial(_conformer_kernel, n_blocks=n_blocks, width=width, rows=rows),
        grid=(b, n_blocks),
        in_specs=[pl.BlockSpec((n_strips, rows, LANES), lambda bi, c: (0, bi * n_blocks + c, 0)),
                  pl.BlockSpec((n_strips, CONF_HALO, LANES),
                               lambda bi, c: (0, bi * n_hblk + jnp.maximum(c * hb - 1, 0), 0)),
                  pl.BlockSpec((n_strips, CONF_HALO, LANES),
                               lambda bi, c: (0, bi * n_hblk + jnp.minimum(c * hb + hb, n_hblk - 1), 0)),
                  pl.BlockSpec((1, rows, dc), lambda bi, c: (bi, c, 0))]
                 + [whole(a) for a in params],
        out_specs=pl.BlockSpec((1, rows, dc), lambda bi, c: (bi, c, 0)),
        out_shape=jax.ShapeDtypeStruct((b, s, dc), BF16),
        scratch_shapes=[pltpu.VMEM((n_strips, SUBLANES * (seg + 2 * CONF_HALO + SUBLANES), LANES), F32),
                        pltpu.VMEM((n_strips, SUBLANES * (seg + SUBLANES), LANES), F32)],
        compiler_params=_cparams(("parallel", "arbitrary")),
        name="conformer_conv_ln",
    )(v, v, v, g.reshape(b, s, dc), *params)


def _even_layer(h, b, s, e, norm_w, w_in_all, conv_w, conv_b, dt_bias_f, dt_bias_b, a_log_f, a_log_b,
                d_skip, gnorm_w, rpb, w_out_all):
    t, d = h.shape
    heads = d_skip.shape[0]
    conv_ch = conv_w.shape[1]
    groups = (conv_ch - d) // (2 * SSM_STATE)
    assert conv_ch == 2 * d, "SSD conv channels are fetched as two d-wide column blocks"
    na_heads = rpb.shape[0]
    n_dt = 2 * heads
    c_dt = d + conv_ch
    c_q = c_dt + n_dt

    w_ssd = _cast_bf16(w_in_all, e, 0, c_dt, tc=d)
    dt_pad = (-n_dt) % LANES
    w_dt = jnp.pad(w_in_all[e, :, c_dt:c_q], ((0, 0), (0, dt_pad)))
    q_scale = (d // na_heads) ** -0.5
    w_na = _cast_bf16_unaligned(w_in_all, e, c_q, 4 * d, q_scale, tc=d)

    u, dt_raw = _norm_matmul(h, norm_w, w_ssd, BF16, w_side=w_dt)
    u = u.reshape(b, s, -1)
    dt_raw = dt_raw.reshape(b, s, -1)
    assert d // na_heads == LANES, "attention heads are exchanged as 128-lane column strips"
    u_na = _norm_matmul(h, norm_w, w_na, BF16, strips=True, silu_from_col=3 * d)

    d_skip_exp = jnp.repeat(d_skip, d // heads)
    common = dict(heads=heads, groups=groups, d=d)
    y_b, xbc = _ssd_first(u, dt_raw, conv_w, conv_b, dt_bias_b, a_log_b, **common)
    y_ssd = _ssd_final(u, xbc, dt_raw, y_b, dt_bias_f, a_log_f, d_skip_exp, gnorm_w, **common)
    y_na = _na_attention(u_na, rpb, b=b)

    return _proj_residual(h, [y_ssd.reshape(t, d), y_na], _cast_bf16(w_out_all, e))


def _odd_layer(h, b, s, e, norm_w, w_in_all, dw_w, dw_b, ln_w, ln_b, w_out_all):
    v, g = _conformer_in(h, norm_w, _cast_bf16(w_in_all, e))
    y = _conformer_mix(v, g, dw_w, dw_b, ln_w, ln_b, b=b)
    return _proj_residual(h, [y.reshape(h.shape[0], -1)], _cast_bf16(w_out_all, e))


def kernel(x, p, ev_norm_w, ev_w_in, ev_conv_w, ev_conv_b, ev_dt_bias_f, ev_dt_bias_b, ev_a_log_f, ev_a_log_b, ev_d_skip, ev_gnorm_w, ev_rpb, ev_w_out, od_norm_w, od_w_in, od_dw_w, od_dw_b, od_ln_w, od_ln_b, od_w_out, ple_norm_w, ple_w_gate, ple_w_proj, final_norm_w):
    b, s, d = x.shape
    depth = p.shape[0]
    h = x.reshape(b * s, d)
    p_all = p.reshape(depth, b * s, -1)
    for i in range(depth):
        e = i // 2
        if i % 2 == 0:
            h = _even_layer(h, b, s, e, ev_norm_w[e], ev_w_in, ev_conv_w[e], ev_conv_b[e],
                            ev_dt_bias_f[e], ev_dt_bias_b[e], ev_a_log_f[e], ev_a_log_b[e],
                            ev_d_skip[e], ev_gnorm_w[e], ev_rpb[e], ev_w_out)
        else:
            h = _odd_layer(h, b, s, e, od_norm_w[e], od_w_in, od_dw_w[e], od_dw_b[e],
                           od_ln_w[e], od_ln_b[e], od_w_out)
        h = _ple(h, ple_norm_w[i], _cast_bf16(ple_w_gate, i), p_all, i,
                 _cast_bf16(ple_w_proj, i), final_norm_w if i == depth - 1 else None)
    return h.reshape(b, s, d)
```
